```python
import math
import jax
import jax.numpy as jnp
from jax import lax
import numpy as np


D_MODEL = 1024
BATCH = 16
SEQ = 2048
DEPTH = 4

A_HEADS = 8
A_KV_HEADS = 2
A_HEAD_DIM = 64
A_WIDTH = A_HEADS * A_HEAD_DIM
IDX_HEADS = 8
IDX_DIM = 32
TOPK_MAX = 256
Q_BLOCK = 128
ROPE_THETA = 500000.0
ROPE_FRACTION = 4
SC_WIDTH = 512
SC_CONV = 3
RW_HEADS = 8
RW_HEAD_DIM = 64
RW_WIDTH = RW_HEADS * RW_HEAD_DIM
RW_W_LORA = 64
RW_A_LORA = 64
RW_G_LORA = 128
RW_IN = 3 * RW_WIDTH + RW_W_LORA + RW_A_LORA + RW_G_LORA
RW_GN_EPS = 64e-5
N_BRANCH = 3
BRANCH_WIDTH = 512
D_FF = 2816
FFN_CONV = 3
RMS_EPS = 1e-6
POS_OFFSET_MAX = 4096

IN_SIZES = (A_WIDTH, A_KV_HEADS * A_HEAD_DIM, A_KV_HEADS * A_HEAD_DIM,
            IDX_HEADS * IDX_DIM, IDX_DIM, IDX_HEADS,
            SC_WIDTH, SC_WIDTH, SC_WIDTH,
            RW_IN,
            N_BRANCH * D_MODEL)
N_IN = sum(IN_SIZES)

kernel_name = 'hybrid_dsa_shortconv_rwkv7_block'


def rms_norm(x, g):
    xf = x.astype(jnp.float32)
    y = xf * lax.rsqrt(jnp.mean(xf * xf, axis=-1, keepdims=True) + RMS_EPS)
    return (y * g.astype(jnp.float32)).astype(x.dtype)


def partial_rope(x, pos):
    dh = x.shape[-1]
    rot = dh // ROPE_FRACTION
    half = rot // 2
    inv = ROPE_THETA ** (-jnp.arange(half, dtype=jnp.float32) * 2.0 / rot)
    ang = pos.astype(jnp.float32)[:, :, None] * inv
    cos = jnp.cos(ang)[:, :, None, :]
    sin = jnp.sin(ang)[:, :, None, :]
    xf = x.astype(jnp.float32)
    x1 = xf[..., :half]
    x2 = xf[..., half:rot]
    out = jnp.concatenate([x1 * cos - x2 * sin, x2 * cos + x1 * sin, xf[..., rot:]], axis=-1)
    return out.astype(x.dtype)


def causal_dwconv(u, w):
    width = w.shape[0]
    seq = u.shape[1]
    up = jnp.pad(u, ((0, 0), (width - 1, 0), (0, 0)))
    out = up[:, 0:seq] * w[0]
    for j in range(1, width):
        out = out + up[:, j:j + seq] * w[j]
    return out


def dsa_attention(q, k, v, qi, ki, wi):
    bsz, seq = q.shape[0], q.shape[1]
    k_sel = min(TOPK_MAX, seq // 4)
    n_blk = seq // Q_BLOCK
    group = A_HEADS // A_KV_HEADS
    scale = A_HEAD_DIM ** -0.5
    idx_scale = (IDX_HEADS ** -0.5) * (IDX_DIM ** -0.5)
    b_idx = jnp.arange(bsz)[:, None, None]
    key_pos = jnp.arange(seq)

    def block(i):
        t0 = i * Q_BLOCK
        q_b = lax.dynamic_slice_in_dim(q, t0, Q_BLOCK, axis=1)
        qi_b = lax.dynamic_slice_in_dim(qi, t0, Q_BLOCK, axis=1)
        wi_b = lax.dynamic_slice_in_dim(wi, t0, Q_BLOCK, axis=1)
        t_pos = t0 + jnp.arange(Q_BLOCK)
        causal = key_pos[None, :] <= t_pos[:, None]
        rel = jax.nn.relu(jnp.einsum('bthd,bsd->bths', qi_b, ki).astype(jnp.float32))
        isc = jnp.einsum('bths,bth->bts', rel, wi_b.astype(jnp.float32) * idx_scale)
        isc = jnp.where(causal[None], isc, -jnp.inf)
        _, sel = lax.top_k(isc, k_sel)
        valid = sel <= t_pos[None, :, None]
        k_g = k[b_idx, sel]
        v_g = v[b_idx, sel]
        q_g = q_b.reshape(bsz, Q_BLOCK, A_KV_HEADS, group, A_HEAD_DIM)
        s = jnp.einsum('btngd,btknd->btngk', q_g, k_g).astype(jnp.float32) * scale
        s = jnp.where(valid[:, :, None, None, :], s, -jnp.inf)
        p = jax.nn.softmax(s, axis=-1).astype(v.dtype)
        o = jnp.einsum('btngk,btknd->btngd', p, v_g)
        return o.reshape(bsz, Q_BLOCK, A_WIDTH)

    out = lax.map(block, jnp.arange(n_blk))
    return out.transpose(1, 0, 2, 3).reshape(bsz, seq, A_WIDTH)


def short_conv_mixer(u, sc_b, sc_c, w_conv):
    return sc_b * causal_dwconv(sc_c * u, w_conv)


def rwkv7_time_mix(z, mu, w0, w_up, a0, a_up, g_up, k_k, k_a, r_k, ln_w, ln_b):
    bsz, seq = z.shape[0], z.shape[1]
    f32 = jnp.float32
    z_prev = jnp.pad(z, ((0, 0), (1, 0), (0, 0)))[:, :seq]
    z = z + (z_prev - z) * mu
    r, k, v, lw, la, lg = jnp.split(
        z, [RW_WIDTH, 2 * RW_WIDTH, 3 * RW_WIDTH, 3 * RW_WIDTH + RW_W_LORA,
            3 * RW_WIDTH + RW_W_LORA + RW_A_LORA], axis=-1)
    zw = (w0 + jnp.tanh(lw) @ w_up).astype(f32)
    decay = jnp.exp(-math.exp(-0.5) * jax.nn.sigmoid(zw))
    a = jax.nn.sigmoid((a0 + la @ a_up).astype(f32))
    g = (jax.nn.sigmoid(lg) @ g_up).astype(f32)
    hs = (bsz, seq, RW_HEADS, RW_HEAD_DIM)
    r = r.astype(f32).reshape(hs)
    v = v.astype(f32).reshape(hs)
    kf = k.astype(f32)
    decay = decay.reshape(hs)
    a = a.reshape(hs)
    kk = (kf * k_k.astype(f32)).reshape(hs)
    kk = kk * lax.rsqrt(jnp.maximum(jnp.sum(kk * kk, axis=-1, keepdims=True), 1e-24))
    kf = (kf * (1.0 + (a.reshape(bsz, seq, RW_WIDTH) - 1.0) * k_a.astype(f32))).reshape(hs)

    def step(state, inp):
        r_t, w_t, k_t, v_t, ka_t, kb_t = inp
        sa = jnp.einsum('bhvk,bhk->bhv', state, ka_t)
        state = (state * w_t[:, :, None, :] + sa[..., None] * kb_t[:, :, None, :]
                 + v_t[..., None] * k_t[:, :, None, :])
        y_t = jnp.einsum('bhvk,bhk->bhv', state, r_t)
        return state, y_t

    tm = lambda t: jnp.swapaxes(t, 0, 1)
    xs = (tm(r), tm(decay), tm(kf), tm(v), tm(-kk), tm(kk * a))
    s0 = jnp.zeros((bsz, RW_HEADS, RW_HEAD_DIM, RW_HEAD_DIM), f32)
    _, y = lax.scan(step, s0, xs)
    y = jnp.swapaxes(y, 0, 1)
    mean = jnp.mean(y, axis=-1, keepdims=True)
    var = jnp.mean(jnp.square(y - mean), axis=-1, keepdims=True)
    yn = ((y - mean) * lax.rsqrt(var + RW_GN_EPS) * ln_w.astype(f32).reshape(RW_HEADS, RW_HEAD_DIM)
          + ln_b.astype(f32).reshape(RW_HEADS, RW_HEAD_DIM))
    bonus = jnp.sum(r * kf * r_k.astype(f32), axis=-1, keepdims=True) * v
    out = (yn + bonus) * g.reshape(hs)
    return out.reshape(bsz, seq, RW_WIDTH).astype(z.dtype)


def conv_glu_ffn(x, w_up, w_conv, w_down):
    u = causal_dwconv(x @ w_up, w_conv)
    gate, up = jnp.split(u, 2, axis=-1)
    return (jax.nn.silu(gate) * up) @ w_down


def setup_inputs(seed: int = 0) -> dict:
    key = jax.random.key(seed)
    ks = jax.random.split(key, 26)
    f32 = jnp.float32

    def nrm(k, shape, scale):
        return jax.random.normal(k, shape, f32) * scale

    L = DEPTH
    x = nrm(ks[0], (BATCH, SEQ, D_MODEL), 1.0)
    positions = (jax.random.randint(ks[1], (BATCH, 1), 0, POS_OFFSET_MAX, dtype=jnp.int32)
                 + jnp.arange(SEQ, dtype=jnp.int32)[None, :])
    return {
        'x': x,
        'positions': positions,
        'norm_mix': 1.0 + nrm(ks[2], (L, D_MODEL), 0.02),
        'w_in': nrm(ks[3], (L, D_MODEL, N_IN), D_MODEL ** -0.5),
        'b_gate': nrm(ks[4], (L, N_BRANCH * D_MODEL), 0.02),
        'sc_conv': nrm(ks[5], (L, SC_CONV, SC_WIDTH), SC_CONV ** -0.5),
        'rw_mu': jax.random.uniform(ks[6], (L, RW_IN), f32),
        'rw_w0': -1.0 + nrm(ks[7], (L, RW_WIDTH), 0.5),
        'rw_w_up': nrm(ks[8], (L, RW_W_LORA, RW_WIDTH), 0.5 * RW_W_LORA ** -0.5),
        'rw_a0': nrm(ks[9], (L, RW_WIDTH), 0.1),
        'rw_a_up': nrm(ks[10], (L, RW_A_LORA, RW_WIDTH), 0.5 * RW_A_LORA ** -0.5),
        'rw_g_up': nrm(ks[11], (L, RW_G_LORA, RW_WIDTH), RW_G_LORA ** -0.5),
        'rw_k_k': 0.85 + nrm(ks[12], (L, RW_WIDTH), 0.05),
        'rw_k_a': 1.0 + nrm(ks[13], (L, RW_WIDTH), 0.05),
        'rw_r_k': nrm(ks[14], (L, RW_HEADS, RW_HEAD_DIM), 0.1),
        'rw_ln_w': 1.0 + nrm(ks[15], (L, RW_WIDTH), 0.02),
        'rw_ln_b': nrm(ks[16], (L, RW_WIDTH), 0.02),
        'w_branch': nrm(ks[17], (L, N_BRANCH, BRANCH_WIDTH, D_MODEL), BRANCH_WIDTH ** -0.5),
        'w_out': nrm(ks[18], (L, D_MODEL, D_MODEL), D_MODEL ** -0.5),
        'norm_ffn': 1.0 + nrm(ks[19], (L, D_MODEL), 0.02),
        'ffn_up': nrm(ks[20], (L, D_MODEL, 2 * D_FF), D_MODEL ** -0.5),
        'ffn_conv': nrm(ks[21], (L, FFN_CONV, 2 * D_FF), FFN_CONV ** -0.5),
        'ffn_down': nrm(ks[22], (L, D_FF, D_MODEL), D_FF ** -0.5),
        'norm_final': 1.0 + nrm(ks[23], (D_MODEL,), 0.02),
    }


def reference(x, positions, norm_mix, w_in, b_gate, sc_conv, rw_mu, rw_w0, rw_w_up, rw_a0,
              rw_a_up, rw_g_up, rw_k_k, rw_k_a, rw_r_k, rw_ln_w, rw_ln_b, w_branch, w_out,
              norm_ffn, ffn_up, ffn_conv, ffn_down, norm_final):
    bsz, seq = x.shape[0], x.shape[1]
    split_points = [int(p) for p in np.cumsum(IN_SIZES)[:-1]]
    h = x
    for l in range(DEPTH):
        xn = rms_norm(h, norm_mix[l])
        proj = xn @ w_in[l]
        (q, k, v, qi, ki, wi, sc_u, sc_b, sc_c, rw_z, gate_pre) = jnp.split(proj, split_points, axis=-1)
        q = partial_rope(q.reshape(bsz, seq, A_HEADS, A_HEAD_DIM), positions)
        k = partial_rope(k.reshape(bsz, seq, A_KV_HEADS, A_HEAD_DIM), positions)
        v = v.reshape(bsz, seq, A_KV_HEADS, A_HEAD_DIM)
        qi = partial_rope(qi.reshape(bsz, seq, IDX_HEADS, IDX_DIM), positions)
        ki = partial_rope(ki.reshape(bsz, seq, 1, IDX_DIM), positions)[:, :, 0]
        y_a = dsa_attention(q, k, v, qi, ki, wi)
        y_b = short_conv_mixer(sc_u, sc_b, sc_c, sc_conv[l])
        y_c = rwkv7_time_mix(rw_z, rw_mu[l], rw_w0[l], rw_w_up[l], rw_a0[l], rw_a_up[l],
                             rw_g_up[l], rw_k_k[l], rw_k_a[l], rw_r_k[l], rw_ln_w[l], rw_ln_b[l])
        branches = jnp.stack([y_a, y_b, y_c], axis=2)
        up = jnp.einsum('bsgc,gcd->bsgd', branches, w_branch[l])
        gates = jax.nn.sigmoid(gate_pre + b_gate[l]).reshape(bsz, seq, N_BRANCH, D_MODEL)
        mixed = jnp.sum(gates * up, axis=2)
        h = h + mixed @ w_out[l]
        h = h + conv_glu_ffn(rms_norm(h, norm_ffn[l]), ffn_up[l], ffn_conv[l], ffn_down[l])
    return rms_norm(h, norm_final)
```

```python
import functools
import math

import jax
import jax.numpy as jnp
from jax import lax
from jax.experimental import pallas as pl
from jax.experimental.pallas import tpu as pltpu

F32 = jnp.float32
BF16 = jnp.bfloat16
HIGHEST = lax.Precision.HIGHEST

D_MODEL = 1024
A_HEADS = 8
A_KV_HEADS = 2
A_HEAD_DIM = 64
A_WIDTH = A_HEADS * A_HEAD_DIM
A_KV_WIDTH = A_KV_HEADS * A_HEAD_DIM
IDX_HEADS = 8
IDX_DIM = 32
IDX_WIDTH = IDX_HEADS * IDX_DIM
TOPK_MAX = 256
ROPE_THETA = 500000.0
ROPE_FRACTION = 4
SC_WIDTH = 512
RW_HEADS = 8
RW_HEAD_DIM = 64
RW_WIDTH = RW_HEADS * RW_HEAD_DIM
RW_W_LORA = 64
RW_A_LORA = 64
RW_G_LORA = 128
RW_IN = 3 * RW_WIDTH + RW_W_LORA + RW_A_LORA + RW_G_LORA
RW_GN_EPS = 64e-5
RW_DECAY_SCALE = math.exp(-0.5)
N_BRANCH = 3
D_FF = 2816
RMS_EPS = 1e-6

LANES = 128
SUBLANES = 8
INT_MIN = -2147483648
VMEM_LIMIT = 48 * 1024 * 1024

ROW_TILE = 512
DSA_Q_TILE = 128
DSA_CLASSES = 4
RW_CHUNK = 64
FF_TILE = D_FF // 2


def _cparams(sem):
    return pltpu.CompilerParams(dimension_semantics=sem, vmem_limit_bytes=VMEM_LIMIT)


def _rms(x, g):
    ms = jnp.mean(x * x, axis=-1, keepdims=True)
    return (x * lax.rsqrt(ms + RMS_EPS) * g).astype(BF16)


def _dot(a, b, precision=None):
    return jnp.dot(a, b, preferred_element_type=F32, precision=precision)


def _dot_nt(a, b, precision=None):
    return lax.dot_general(a, b, (((1,), (1,)), ((), ())), preferred_element_type=F32,
                           precision=precision)


def _dot_tn(a, b, precision=None):
    return lax.dot_general(a, b, (((0,), (0,)), ((), ())), preferred_element_type=F32,
                           precision=precision)


def _shift_rows(x, prev, k):
    xr = pltpu.roll(x, k, 0)
    pr = pltpu.roll(prev, k, 0)
    rows = lax.broadcasted_iota(jnp.int32, pr.shape, 0)
    head = jnp.where(rows < k, pr, xr[0:SUBLANES])
    return jnp.concatenate([head, xr[SUBLANES:]], axis=0)


def _rope_slab(y, c, s1, s2, half):
    return y * c + pltpu.roll(y, LANES - half, 1) * s1 + pltpu.roll(y, half, 1) * s2


def _attn_proj_kernel(h_ref, g_ref, w_ref, cq_ref, s1q_ref, s2q_ref, ci_ref, s1i_ref, s2i_ref,
                      q_ref, k_ref, v_ref, qi_ref, kw_ref):
    xn = _rms(h_ref[0], g_ref[...])
    cq, s1q, s2q = cq_ref[0], s1q_ref[0], s2q_ref[0]
    ci, s1i, s2i = ci_ref[0], s1i_ref[0], s2i_ref[0]
    hq = A_HEAD_DIM // ROPE_FRACTION // 2
    hi = IDX_DIM // ROPE_FRACTION // 2
    col = 0
    for p in range(A_WIDTH // LANES):
        y = _dot(xn, w_ref[:, col:col + LANES])
        q_ref[0, :, p * LANES:(p + 1) * LANES] = _rope_slab(y, cq, s1q, s2q, hq).astype(BF16)
        col += LANES
    y = _dot(xn, w_ref[:, col:col + LANES])
    k_ref[0] = _rope_slab(y, cq, s1q, s2q, hq).astype(BF16)
    col += LANES
    v_ref[0] = _dot(xn, w_ref[:, col:col + LANES]).astype(BF16)
    col += LANES
    for p in range(IDX_WIDTH // LANES):
        y = _dot(xn, w_ref[:, col:col + LANES])
        qi_ref[0, :, p * LANES:(p + 1) * LANES] = _rope_slab(y, ci, s1i, s2i, hi).astype(BF16)
        col += LANES
    y = _dot(xn, w_ref[:, col:col + LANES])
    lane = lax.broadcasted_iota(jnp.int32, y.shape, 1)
    is_key = lane < IDX_DIM
    ck = jnp.where(is_key, ci, 1.0)
    kw_ref[0] = _rope_slab(y, ck, jnp.where(is_key, s1i, 0.0), jnp.where(is_key, s2i, 0.0), hi)


def _attn_proj(h, g, w, tabs, ts):
    b, s, d = h.shape
    n = w.shape[1]
    row = lambda i, j: (i, j, 0)
    const = lambda i, j: (0, 0)
    tab_spec = pl.BlockSpec((1, ts, LANES), row)
    return pl.pallas_call(
        _attn_proj_kernel,
        grid=(b, s // ts),
        in_specs=[pl.BlockSpec((1, ts, d), row), pl.BlockSpec((1, d), const),
                  pl.BlockSpec((d, n), const)] + [tab_spec] * 6,
        out_specs=[pl.BlockSpec((1, ts, A_WIDTH), row), pl.BlockSpec((1, ts, LANES), row),
                   pl.BlockSpec((1, ts, LANES), row), pl.BlockSpec((1, ts, IDX_WIDTH), row),
                   pl.BlockSpec((1, ts, LANES), row)],
        out_shape=[jax.ShapeDtypeStruct((b, s, A_WIDTH), BF16),
                   jax.ShapeDtypeStruct((b, s, LANES), BF16),
                   jax.ShapeDtypeStruct((b, s, LANES), BF16),
                   jax.ShapeDtypeStruct((b, s, IDX_WIDTH), BF16),
                   jax.ShapeDtypeStruct((b, s, LANES), F32)],
        compiler_params=_cparams(("arbitrary", "arbitrary")),
        name="attn_proj",
    )(h, g, w, *tabs)


def _sc_proj_kernel(h_ref, g_ref, w_ref, cw_ref, y_ref, carry_ref):
    @pl.when(pl.program_id(1) == 0)
    def _():
        carry_ref[...] = jnp.zeros_like(carry_ref)

    xn = _rms(h_ref[0], g_ref[...])
    ts = xn.shape[0]
    for p in range(SC_WIDTH // LANES):
        lo = p * LANES
        u = _dot(xn, w_ref[:, lo:lo + LANES])
        gate_b = _dot(xn, w_ref[:, SC_WIDTH + lo:SC_WIDTH + lo + LANES])
        gate_c = _dot(xn, w_ref[:, 2 * SC_WIDTH + lo:2 * SC_WIDTH + lo + LANES])
        cu = gate_c * u
        prev = carry_ref[:, lo:lo + LANES]
        conv = (_shift_rows(cu, prev, 2) * cw_ref[0:1, lo:lo + LANES]
                + _shift_rows(cu, prev, 1) * cw_ref[1:2, lo:lo + LANES]
                + cu * cw_ref[2:3, lo:lo + LANES])
        y_ref[0, :, lo:lo + LANES] = (gate_b * conv).astype(BF16)
        carry_ref[:, lo:lo + LANES] = cu[ts - SUBLANES:ts]


def _sc_proj(h, g, w, cw, ts):
    b, s, d = h.shape
    row = lambda i, j: (i, j, 0)
    const = lambda i, j: (0, 0)
    return pl.pallas_call(
        _sc_proj_kernel,
        grid=(b, s // ts),
        in_specs=[pl.BlockSpec((1, ts, d), row), pl.BlockSpec((1, d), const),
                  pl.BlockSpec((d, 3 * SC_WIDTH), const), pl.BlockSpec((3, SC_WIDTH), const)],
        out_specs=pl.BlockSpec((1, ts, SC_WIDTH), row),
        out_shape=jax.ShapeDtypeStruct((b, s, SC_WIDTH), BF16),
        scratch_shapes=[pltpu.VMEM((SUBLANES, SC_WIDTH), F32)],
        compiler_params=_cparams(("arbitrary", "arbitrary")),
        name="sc_proj",
    )(h, g, w, cw)


def _split_dot(x, m):
    hi = x.astype(BF16)
    lo = (x - hi.astype(F32)).astype(BF16)
    return _dot(hi, m) + _dot(lo, m)


def _rw_proj_kernel(h_ref, g_ref, w_ref, mu_ref, wlora_ref, gup_ref, w0_ref, a0_ref, kk_ref,
                    ka_ref, hsum_ref,
                    r_ref, lw_ref, k_ref, v_ref, kn_ref, a_ref, gg_ref, carry_ref):
    @pl.when(pl.program_id(1) == 0)
    def _():
        carry_ref[...] = jnp.zeros_like(carry_ref)

    xn = _rms(h_ref[0], g_ref[...])
    ts = xn.shape[0]

    def mixed(lo, width):
        z = _dot(xn, w_ref[:, lo:lo + width])
        zp = _shift_rows(z, carry_ref[:, lo:lo + width], 1)
        carry_ref[:, lo:lo + width] = z[ts - SUBLANES:ts]
        return z + (zp - z) * mu_ref[:, lo:lo + width]

    lora_lo = 3 * RW_WIDTH
    zl = mixed(lora_lo, LANES)
    lane = lax.broadcasted_iota(jnp.int32, zl.shape, 1)
    zl = jnp.where(lane < RW_W_LORA, jnp.tanh(zl), zl).astype(BF16)
    zg = jax.nn.sigmoid(mixed(lora_lo + LANES, RW_G_LORA)).astype(BF16)
    for p in range(RW_WIDTH // LANES):
        lo = p * LANES
        sl = slice(lo, lo + LANES)
        up = _dot(zl, wlora_ref[:, lo:lo + LANES])
        ua = _dot(zl, wlora_ref[:, RW_WIDTH + lo:RW_WIDTH + lo + LANES])
        lw_ref[0, :, sl] = -RW_DECAY_SCALE * jax.nn.sigmoid(w0_ref[:, sl] + up)
        a = jax.nn.sigmoid(a0_ref[:, sl] + ua)
        a_ref[0, :, sl] = a
        gg_ref[0, :, sl] = _dot(zg, gup_ref[:, sl])
        r_ref[0, :, sl] = mixed(lo, LANES)
        k = mixed(RW_WIDTH + lo, LANES)
        v_ref[0, :, sl] = mixed(2 * RW_WIDTH + lo, LANES)
        kk = k * kk_ref[:, sl]
        ss = _split_dot(kk * kk, hsum_ref[...])
        kn_ref[0, :, sl] = kk * lax.rsqrt(jnp.maximum(ss, 1e-24))
        k_ref[0, :, sl] = k * (1.0 + (a - 1.0) * ka_ref[:, sl])


def _rw_proj(h, g, w, mu, wlora, gup, w0, a0, k_k, k_a, hsum, ts):
    b, s, d = h.shape
    row = lambda i, j: (i, j, 0)
    const = lambda i, j: (0, 0)
    vec = pl.BlockSpec((1, RW_WIDTH), const)
    out_spec = pl.BlockSpec((1, ts, RW_WIDTH), row)
    out_shape = jax.ShapeDtypeStruct((b, s, RW_WIDTH), F32)
    return pl.pallas_call(
        _rw_proj_kernel,
        grid=(b, s // ts),
        in_specs=[pl.BlockSpec((1, ts, d), row), pl.BlockSpec((1, d), const),
                  pl.BlockSpec((d, RW_IN), const), pl.BlockSpec((1, RW_IN), const),
                  pl.BlockSpec((LANES, 2 * RW_WIDTH), const),
                  pl.BlockSpec((RW_G_LORA, RW_WIDTH), const),
                  vec, vec, vec, vec, pl.BlockSpec((LANES, LANES), const)],
        out_specs=[out_spec] * 7,
        out_shape=[out_shape] * 7,
        scratch_shapes=[pltpu.VMEM((SUBLANES, RW_IN), F32)],
        compiler_params=_cparams(("arbitrary", "arbitrary")),
        name="rw_proj",
    )(h, g, w, mu, wlora, gup, w0, a0, k_k, k_a, hsum)


def _gate_proj_kernel(h_ref, g_ref, w_ref, b_ref, o_ref):
    xn = _rms(h_ref[0], g_ref[...])
    n = w_ref.shape[1]
    step = 4 * LANES
    for lo in range(0, n, step):
        y = _dot(xn, w_ref[:, lo:lo + step]) + b_ref[:, lo:lo + step]
        o_ref[0, :, lo:lo + step] = jax.nn.sigmoid(y).astype(BF16)


def _gate_proj(h, g, w, bias, ts):
    b, s, d = h.shape
    n = w.shape[1]
    row = lambda i, j: (i, j, 0)
    const = lambda i, j: (0, 0)
    return pl.pallas_call(
        _gate_proj_kernel,
        grid=(b, s // ts),
        in_specs=[pl.BlockSpec((1, ts, d), row), pl.BlockSpec((1, d), const),
                  pl.BlockSpec((d, n), const), pl.BlockSpec((1, n), const)],
        out_specs=pl.BlockSpec((1, ts, n), row),
        out_shape=jax.ShapeDtypeStruct((b, s, n), BF16),
        compiler_params=_cparams(("arbitrary", "arbitrary")),
        name="gate_proj",
    )(h, g, w, bias)


def _dsa_kernel(q_ref, k_ref, v_ref, qi_ref, kwk_ref, kwq_ref, o_ref, key_ref, bias_ref,
                *, first_block, k_sel):
    tq = q_ref.shape[1]
    sk = k_ref.shape[1]
    t0 = (first_block + pl.program_id(1)) * tq

    ki = kwk_ref[0][:, 0:IDX_DIM].astype(BF16)
    wq = kwq_ref[0]
    qi = qi_ref[0]
    isc = jnp.zeros((tq, sk), F32)
    for h in range(IDX_HEADS):
        rel = jnp.maximum(_dot_nt(qi[:, h * IDX_DIM:(h + 1) * IDX_DIM], ki), 0.0)
        isc = isc + wq[:, IDX_DIM + h:IDX_DIM + h + 1] * rel

    cols = lax.broadcasted_iota(jnp.int32, (tq, sk), 1)
    rows = t0 + lax.broadcasted_iota(jnp.int32, (tq, sk), 0)
    causal = cols <= rows
    bits = pltpu.bitcast(isc, jnp.int32)
    key = bits ^ ((bits >> 31) & jnp.int32(0x7FFFFFFF))
    key_ref[...] = jnp.where(causal, key, INT_MIN)

    def search(it, tau_u):
        bit = jnp.left_shift(jnp.int32(1), 31 - it)
        cand_u = tau_u | bit
        cand_s = cand_u ^ INT_MIN
        cnt = jnp.sum(jnp.where(key_ref[...] >= cand_s, 1.0, 0.0), axis=1, keepdims=True)
        return jnp.where(cnt >= k_sel, cand_u, tau_u)

    tau = lax.fori_loop(0, 32, search, jnp.zeros((tq, 1), jnp.int32)) ^ INT_MIN

    need = k_sel - jnp.sum(jnp.where(key_ref[...] > tau, 1.0, 0.0), axis=1, keepdims=True)
    r2 = lax.broadcasted_iota(jnp.int32, (LANES, 2 * LANES), 0)
    c2 = lax.broadcasted_iota(jnp.int32, (LANES, 2 * LANES), 1)
    tri = jnp.where((r2 < c2) | (c2 >= LANES), 1.0, 0.0).astype(BF16)
    carry = jnp.zeros((tq, LANES), F32)
    rows_j = t0 + lax.broadcasted_iota(jnp.int32, (tq, LANES), 0)
    cols_j = lax.broadcasted_iota(jnp.int32, (tq, LANES), 1)
    for j in range(sk // LANES):
        sl = slice(j * LANES, (j + 1) * LANES)
        key_j = key_ref[:, sl]
        eq_f = jnp.where(key_j == tau, 1.0, 0.0)
        pr = _dot(eq_f.astype(BF16), tri)
        take = jnp.where(carry + pr[:, :LANES] < need, eq_f, 0.0)
        sel = jnp.where(key_j > tau, 1.0, take)
        sel = jnp.where(cols_j + j * LANES <= rows_j, sel, 0.0)
        bias_ref[:, sl] = jnp.where(sel > 0.0, 0.0, -jnp.inf)
        carry = carry + pr[:, LANES:]

    bias = bias_ref[...]
    kk = k_ref[0]
    vv = v_ref[0]
    group = A_HEADS // A_KV_HEADS
    for h in range(A_HEADS):
        n = h // group
        kn = kk[:, n * A_HEAD_DIM:(n + 1) * A_HEAD_DIM]
        vn = vv[:, n * A_HEAD_DIM:(n + 1) * A_HEAD_DIM]
        s = _dot_nt(q_ref[0, :, h * A_HEAD_DIM:(h + 1) * A_HEAD_DIM], kn) + bias
        m = jnp.max(s, axis=1, keepdims=True)
        p = jnp.exp(s - m)
        l = jnp.sum(p, axis=1, keepdims=True)
        o = _dot(p.astype(BF16), vn) / l
        o_ref[0, :, h * A_HEAD_DIM:(h + 1) * A_HEAD_DIM] = o.astype(BF16)


def _dsa(q, k, v, qi, kw, k_sel):
    b, s, _ = q.shape
    tq = min(DSA_Q_TILE, s)
    nblk = s // tq
    per = max(1, nblk // DSA_CLASSES)
    outs = []
    for first in range(0, nblk, per):
        sk = (first + per) * tq
        qmap = lambda i, j, first=first: (i, first + j, 0)
        kmap = lambda i, j: (i, 0, 0)
        outs.append(pl.pallas_call(
            functools.partial(_dsa_kernel, first_block=first, k_sel=k_sel),
            grid=(b, per),
            in_specs=[pl.BlockSpec((1, tq, A_WIDTH), qmap), pl.BlockSpec((1, sk, LANES), kmap),
                      pl.BlockSpec((1, sk, LANES), kmap), pl.BlockSpec((1, tq, IDX_WIDTH), qmap),
                      pl.BlockSpec((1, sk, LANES), kmap), pl.BlockSpec((1, tq, LANES), qmap)],
            out_specs=pl.BlockSpec((1, tq, A_WIDTH), lambda i, j: (i, j, 0)),
            out_shape=jax.ShapeDtypeStruct((b, per * tq, A_WIDTH), BF16),
            scratch_shapes=[pltpu.VMEM((tq, sk), jnp.int32), pltpu.VMEM((tq, sk), F32)],
            compiler_params=_cparams(("arbitrary", "arbitrary")),
            name=f"dsa_{sk}",
        )(q, k, v, qi, kw, kw))
    return jnp.concatenate(outs, axis=1)


def _rwkv_kernel(r_ref, lw_ref, k_ref, v_ref, kn_ref, a_ref, g_ref, rk_ref, lnw_ref, lnb_ref,
                 y_ref, st_ref):
    @pl.when(pl.program_id(1) == 0)
    def _():
        st_ref[...] = jnp.zeros_like(st_ref)

    r, lw, kf, v = r_ref[0], lw_ref[0], k_ref[0], v_ref[0]
    kn, a, g = kn_ref[0], a_ref[0], g_ref[0]
    c = r.shape[0]
    hd = RW_HEAD_DIM
    row = lax.broadcasted_iota(jnp.int32, (c, c), 0)
    col = lax.broadcasted_iota(jnp.int32, (c, c), 1)
    lower = col <= row
    strict = col < row
    eye = jnp.where(col == row, 1.0, 0.0)

    cum = _dot(jnp.where(lower, 1.0, 0.0), lw, precision=HIGHEST)
    cum_last = cum[c - 1:c, :]
    p_inc = jnp.exp(cum)
    p_exc = jnp.exp(cum - lw)
    p_inv = jnp.exp(-cum)
    p_rem = jnp.exp(cum_last - cum)
    p_last = jnp.exp(cum_last)
    kb = kn * a
    a_t = (-kn * p_exc).astype(BF16)
    r_t = (r * p_inc).astype(BF16)
    b_t = (kb * p_inv).astype(BF16)
    k_t = (kf * p_inv).astype(BF16)
    b_h = (kb * p_rem).astype(BF16)
    k_h = (kf * p_rem).astype(BF16)
    v_b = v.astype(BF16)
    bonus_in = r * kf * rk_ref[...]

    for h in range(RW_HEADS):
        sl = slice(h * hd, (h + 1) * hd)
        st = st_ref[h]
        st_b = st.astype(BF16)
        gram = _dot_nt(jnp.concatenate([a_t[:, sl], r_t[:, sl]], axis=0),
                       jnp.concatenate([b_t[:, sl], k_t[:, sl]], axis=0))
        a_ab = jnp.where(strict, gram[:c, :c], 0.0)
        a_ak = jnp.where(strict, gram[:c, c:], 0.0)
        a_rb = jnp.where(lower, gram[c:, :c], 0.0)
        a_rk = jnp.where(lower, gram[c:, c:], 0.0)
        inv = eye + a_ab
        npow = a_ab
        for _ in range(int(math.log2(c)) - 1):
            npow = _dot(npow, npow, precision=HIGHEST)
            inv = inv + _dot(inv, npow, precision=HIGHEST)
        vh = v_b[:, sl]
        rhs = _dot_nt(a_t[:, sl], st_b) + _dot(a_ak.astype(BF16), vh)
        sa = _dot(inv, rhs, precision=HIGHEST)
        sa_b = sa.astype(BF16)
        y = (_dot_nt(r_t[:, sl], st_b) + _dot(a_rb.astype(BF16), sa_b)
             + _dot(a_rk.astype(BF16), vh))
        st_ref[h] = (st * p_last[:, sl] + _dot_tn(sa_b, b_h[:, sl]) + _dot_tn(vh, k_h[:, sl]))
        mean = jnp.mean(y, axis=1, keepdims=True)
        yc = y - mean
        var = jnp.mean(yc * yc, axis=1, keepdims=True)
        yn = yc * lax.rsqrt(var + RW_GN_EPS) * lnw_ref[:, sl] + lnb_ref[:, sl]
        bonus = jnp.sum(bonus_in[:, sl], axis=1, keepdims=True) * v[:, sl]
        y_ref[0, :, sl] = ((yn + bonus) * g[:, sl]).astype(BF16)


def _rwkv(r, lw, kf, v, kn, a, g, r_k, ln_w, ln_b):
    b, s, w = r.shape
    c = min(RW_CHUNK, s)
    row = lambda i, j: (i, j, 0)
    const = lambda i, j: (0, 0)
    seq = pl.BlockSpec((1, c, w), row)
    vec = pl.BlockSpec((1, w), const)
    return pl.pallas_call(
        _rwkv_kernel,
        grid=(b, s // c),
        in_specs=[seq] * 7 + [vec] * 3,
        out_specs=seq,
        out_shape=jax.ShapeDtypeStruct((b, s, w), BF16),
        scratch_shapes=[pltpu.VMEM((RW_HEADS, RW_HEAD_DIM, RW_HEAD_DIM), F32)],
        compiler_params=_cparams(("arbitrary", "arbitrary")),
        name="rwkv7",
    )(r, lw, kf, v, kn, a, g, r_k, ln_w, ln_b)


def _merge_kernel(h_ref, ya_ref, yb_ref, yc_ref, gt_ref, wb_ref, wo_ref, o_ref):
    d = h_ref.shape[2]
    mixed = None
    for i, y_ref in enumerate((ya_ref, yb_ref, yc_ref)):
        up = _dot(y_ref[0], wb_ref[i])
        term = gt_ref[0, :, i * d:(i + 1) * d].astype(F32) * up
        mixed = term if mixed is None else mixed + term
    o_ref[0] = h_ref[0] + _dot(mixed.astype(BF16), wo_ref[...])


def _merge(h, ya, yb, yc, gates, wb, wo, ts):
    b, s, d = h.shape
    row = lambda i, j: (i, j, 0)
    bw = ya.shape[2]
    return pl.pallas_call(
        _merge_kernel,
        grid=(b, s // ts),
        in_specs=[pl.BlockSpec((1, ts, d), row)] + [pl.BlockSpec((1, ts, bw), row)] * 3
                 + [pl.BlockSpec((1, ts, N_BRANCH * d), row),
                    pl.BlockSpec((N_BRANCH, bw, d), lambda i, j: (0, 0, 0)),
                    pl.BlockSpec((d, d), lambda i, j: (0, 0))],
        out_specs=pl.BlockSpec((1, ts, d), row),
        out_shape=jax.ShapeDtypeStruct((b, s, d), F32),
        compiler_params=_cparams(("arbitrary", "arbitrary")),
        name="merge",
    )(h, ya, yb, yc, gates, wb, wo)


def _ffn_kernel(h_ref, g_ref, wg_ref, wu_ref, cg_ref, cu_ref, wd_ref, o_ref,
                xn_ref, acc_ref, act_ref, carry_g_ref, carry_u_ref):
    s_idx = pl.program_id(1)
    f = pl.program_id(2)
    ts = h_ref.shape[1]
    tf = wg_ref.shape[1]

    @pl.when(f == 0)
    def _():
        xn_ref[...] = _rms(h_ref[0], g_ref[...])
        acc_ref[...] = jnp.zeros_like(acc_ref)

    @pl.when(s_idx == 0)
    def _():
        carry_g_ref[f] = jnp.zeros(carry_g_ref.shape[1:], F32)
        carry_u_ref[f] = jnp.zeros(carry_u_ref.shape[1:], F32)

    xn = xn_ref[...]

    def conv(w_ref, c_ref, carry_ref, sl):
        u = _dot(xn, w_ref[:, sl])
        prev = carry_ref[f, :, sl]
        out = (_shift_rows(u, prev, 2) * c_ref[0:1, sl] + _shift_rows(u, prev, 1) * c_ref[1:2, sl]
               + u * c_ref[2:3, sl])
        carry_ref[f, :, sl] = u[ts - SUBLANES:ts]
        return out

    step = 2 * LANES
    for lo in range(0, tf, step):
        sl = slice(lo, min(lo + step, tf))
        gate = conv(wg_ref, cg_ref, carry_g_ref, sl)
        up = conv(wu_ref, cu_ref, carry_u_ref, sl)
        act_ref[:, sl] = (gate * jax.nn.sigmoid(gate) * up).astype(BF16)
    acc_ref[...] += _dot(act_ref[...], wd_ref[...])

    @pl.when(f == pl.num_programs(2) - 1)
    def _():
        o_ref[0] = h_ref[0] + acc_ref[...]


def _ffn(h, g, w_up, w_conv, w_down, ts, tf):
    b, s, d = h.shape
    dff = w_down.shape[0]
    nf = dff // tf
    row = lambda i, j, f: (i, j, 0)
    return pl.pallas_call(
        _ffn_kernel,
        grid=(b, s // ts, nf),
        in_specs=[pl.BlockSpec((1, ts, d), row), pl.BlockSpec((1, d), lambda i, j, f: (0, 0)),
                  pl.BlockSpec((d, tf), lambda i, j, f: (0, f)),
                  pl.BlockSpec((d, tf), lambda i, j, f: (0, nf + f)),
                  pl.BlockSpec((3, tf), lambda i, j, f: (0, f)),
                  pl.BlockSpec((3, tf), lambda i, j, f: (0, nf + f)),
                  pl.BlockSpec((tf, d), lambda i, j, f: (f, 0))],
        out_specs=pl.BlockSpec((1, ts, d), row),
        out_shape=jax.ShapeDtypeStruct((b, s, d), F32),
        scratch_shapes=[pltpu.VMEM((ts, d), BF16), pltpu.VMEM((ts, d), F32),
                        pltpu.VMEM((ts, tf), BF16),
                        pltpu.VMEM((nf, SUBLANES, tf), F32), pltpu.VMEM((nf, SUBLANES, tf), F32)],
        compiler_params=_cparams(("arbitrary", "arbitrary", "arbitrary")),
        name="conv_glu_ffn",
    )(h, g, w_up, w_up, w_conv, w_conv, w_down)


def _final_norm_kernel(h_ref, g_ref, o_ref):
    x = h_ref[0]
    ms = jnp.mean(x * x, axis=-1, keepdims=True)
    o_ref[0] = x * lax.rsqrt(ms + RMS_EPS) * g_ref[...]


def _final_norm(h, g, ts):
    b, s, d = h.shape
    row = lambda i, j: (i, j, 0)
    return pl.pallas_call(
        _final_norm_kernel,
        grid=(b, s // ts),
        in_specs=[pl.BlockSpec((1, ts, d), row), pl.BlockSpec((1, d), lambda i, j: (0, 0))],
        out_specs=pl.BlockSpec((1, ts, d), row),
        out_shape=jax.ShapeDtypeStruct((b, s, d), F32),
        compiler_params=_cparams(("arbitrary", "arbitrary")),
        name="final_norm",
    )(h, g)


def _rope_tables(positions, head_dim):
    rot = head_dim // ROPE_FRACTION
    half = rot // 2
    inv = ROPE_THETA ** (-jnp.arange(half, dtype=F32) * 2.0 / rot)
    ang = positions.astype(F32)[:, :, None] * inv
    cos, sin = jnp.cos(ang), jnp.sin(ang)
    zeros = jnp.zeros_like(cos)
    rest = head_dim - rot
    pad1 = jnp.ones(cos.shape[:2] + (rest,), F32)
    pad0 = jnp.zeros(cos.shape[:2] + (rest,), F32)
    reps = LANES // head_dim
    c = jnp.tile(jnp.concatenate([cos, cos, pad1], axis=-1), (1, 1, reps))
    s1 = jnp.tile(jnp.concatenate([-sin, zeros, pad0], axis=-1), (1, 1, reps))
    s2 = jnp.tile(jnp.concatenate([zeros, sin, pad0], axis=-1), (1, 1, reps))
    return c, s1, s2


def kernel(x, positions, norm_mix, w_in, b_gate, sc_conv, rw_mu, rw_w0, rw_w_up, rw_a0, rw_a_up, rw_g_up, rw_k_k, rw_k_a, rw_r_k, rw_ln_w, rw_ln_b, w_branch, w_out, norm_ffn, ffn_up, ffn_conv, ffn_down, norm_final):
    bsz, seq, d = x.shape
    depth = w_in.shape[0]
    ts = min(ROW_TILE, seq)
    k_sel = min(TOPK_MAX, seq // 4)

    tabs = _rope_tables(positions, A_HEAD_DIM) + _rope_tables(positions, IDX_DIM)
    o_q = 0
    o_k = o_q + A_WIDTH
    o_v = o_k + A_KV_WIDTH
    o_qi = o_v + A_KV_WIDTH
    o_ki = o_qi + IDX_WIDTH
    o_wi = o_ki + IDX_DIM
    o_sc = o_wi + IDX_HEADS
    o_rw = o_sc + 3 * SC_WIDTH
    o_gate = o_rw + RW_IN
    attn_scale = A_HEAD_DIM ** -0.5
    idx_scale = (IDX_HEADS ** -0.5) * (IDX_DIM ** -0.5)
    head_sum = jnp.kron(jnp.eye(LANES // RW_HEAD_DIM, dtype=F32),
                        jnp.ones((RW_HEAD_DIM, RW_HEAD_DIM), F32)).astype(BF16)

    h = x
    for l in range(depth):
        w = w_in[l]
        pad = jnp.zeros((d, LANES - IDX_DIM - IDX_HEADS), F32)
        w_attn = jnp.concatenate(
            [w[:, o_q:o_k] * attn_scale, w[:, o_k:o_ki], w[:, o_ki:o_wi],
             w[:, o_wi:o_sc] * idx_scale, pad], axis=1).astype(BF16)
        w_sc = w[:, o_sc:o_rw].astype(BF16)
        w_rw = w[:, o_rw:o_gate].astype(BF16)
        w_gate = w[:, o_gate:].astype(BF16)
        g_mix = norm_mix[l][None, :]

        q, k, v, qi, kw = _attn_proj(h, g_mix, w_attn, tabs, ts)
        y_b = _sc_proj(h, g_mix, w_sc, sc_conv[l], ts)
        zero_lora = jnp.zeros((RW_W_LORA, RW_WIDTH), F32)
        w_lora = jnp.concatenate(
            [jnp.concatenate([rw_w_up[l], zero_lora], axis=1),
             jnp.concatenate([zero_lora, rw_a_up[l]], axis=1)], axis=0).astype(BF16)
        r, lw, kf, vv, kn, a, gg = _rw_proj(
            h, g_mix, w_rw, rw_mu[l][None, :], w_lora, rw_g_up[l].astype(BF16),
            rw_w0[l][None, :], rw_a0[l][None, :], rw_k_k[l][None, :], rw_k_a[l][None, :],
            head_sum, ts)
        gates = _gate_proj(h, g_mix, w_gate, b_gate[l][None, :], ts)

        y_a = _dsa(q, k, v, qi, kw, k_sel)
        y_c = _rwkv(r, lw, kf, vv, kn, a, gg, rw_r_k[l].reshape(1, RW_WIDTH),
                    rw_ln_w[l][None, :], rw_ln_b[l][None, :])

        h = _merge(h, y_a, y_b, y_c, gates, w_branch[l].astype(BF16), w_out[l].astype(BF16), ts)
        h = _ffn(h, norm_ffn[l][None, :], ffn_up[l].astype(BF16), ffn_conv[l],
                 ffn_down[l].astype(BF16), ts, FF_TILE)
    return _final_norm(h, norm_final[None, :], ts)
```

```python
import functools
import math

import jax
import jax.numpy as jnp
from jax import lax
from jax.experimental import pallas as pl
from jax.experimental.pallas import tpu as pltpu

F32 = jnp.float32
BF16 = jnp.bfloat16
HIGHEST = lax.Precision.HIGHEST

D_MODEL = 1024
A_HEADS = 8
A_KV_HEADS = 2
A_HEAD_DIM = 64
A_WIDTH = A_HEADS * A_HEAD_DIM
A_KV_WIDTH = A_KV_HEADS * A_HEAD_DIM
IDX_HEADS = 8
IDX_DIM = 32
IDX_WIDTH = IDX_HEADS * IDX_DIM
TOPK_MAX = 256
ROPE_THETA = 500000.0
ROPE_FRACTION = 4
SC_WIDTH = 512
RW_HEADS = 8
RW_HEAD_DIM = 64
RW_WIDTH = RW_HEADS * RW_HEAD_DIM
RW_W_LORA = 64
RW_A_LORA = 64
RW_G_LORA = 128
RW_IN = 3 * RW_WIDTH + RW_W_LORA + RW_A_LORA + RW_G_LORA
RW_GN_EPS = 64e-5
RW_DECAY_SCALE = math.exp(-0.5)
N_BRANCH = 3
D_FF = 2816
RMS_EPS = 1e-6

LANES = 128
SUBLANES = 8
INT_MIN = -2147483648
VMEM_LIMIT = 48 * 1024 * 1024

ROW_TILE = 512
DSA_Q_TILE = 128
DSA_CLASSES = 4
RW_CHUNK = 64
FF_TILE = D_FF // 2


def _cparams(sem):
    return pltpu.CompilerParams(dimension_semantics=sem, vmem_limit_bytes=VMEM_LIMIT)


def _rms(x, g):
    ms = jnp.mean(x * x, axis=-1, keepdims=True)
    return (x * lax.rsqrt(ms + RMS_EPS) * g).astype(BF16)


def _dot(a, b, precision=None):
    return jnp.dot(a, b, preferred_element_type=F32, precision=precision)


def _dot_nt(a, b, precision=None):
    return lax.dot_general(a, b, (((1,), (1,)), ((), ())), preferred_element_type=F32,
                           precision=precision)


def _dot_tn(a, b, precision=None):
    return lax.dot_general(a, b, (((0,), (0,)), ((), ())), preferred_element_type=F32,
                           precision=precision)


def _shift_rows(x, prev, k):
    xr = pltpu.roll(x, k, 0)
    pr = pltpu.roll(prev, k, 0)
    rows = lax.broadcasted_iota(jnp.int32, pr.shape, 0)
    head = jnp.where(rows < k, pr, xr[0:SUBLANES])
    return jnp.concatenate([head, xr[SUBLANES:]], axis=0)


def _rope_slab(y, c, s1, s2, half):
    return y * c + pltpu.roll(y, LANES - half, 1) * s1 + pltpu.roll(y, half, 1) * s2


def _attn_proj_kernel(h_ref, g_ref, w_ref, cq_ref, s1q_ref, s2q_ref, ci_ref, s1i_ref, s2i_ref,
                      q_ref, k_ref, v_ref, qi_ref, kw_ref):
    xn = _rms(h_ref[0], g_ref[...])
    cq, s1q, s2q = cq_ref[0], s1q_ref[0], s2q_ref[0]
    ci, s1i, s2i = ci_ref[0], s1i_ref[0], s2i_ref[0]
    hq = A_HEAD_DIM // ROPE_FRACTION // 2
    hi = IDX_DIM // ROPE_FRACTION // 2
    col = 0
    for p in range(A_WIDTH // LANES):
        y = _dot(xn, w_ref[:, col:col + LANES])
        q_ref[0, :, p * LANES:(p + 1) * LANES] = _rope_slab(y, cq, s1q, s2q, hq).astype(BF16)
        col += LANES
    y = _dot(xn, w_ref[:, col:col + LANES])
    k_ref[0] = _rope_slab(y, cq, s1q, s2q, hq).astype(BF16)
    col += LANES
    v_ref[0] = _dot(xn, w_ref[:, col:col + LANES]).astype(BF16)
    col += LANES
    for p in range(IDX_WIDTH // LANES):
        y = _dot(xn, w_ref[:, col:col + LANES])
        qi_ref[0, :, p * LANES:(p + 1) * LANES] = _rope_slab(y, ci, s1i, s2i, hi).astype(BF16)
        col += LANES
    y = _dot(xn, w_ref[:, col:col + LANES])
    lane = lax.broadcasted_iota(jnp.int32, y.shape, 1)
    is_key = lane < IDX_DIM
    ck = jnp.where(is_key, ci, 1.0)
    kw_ref[0] = _rope_slab(y, ck, jnp.where(is_key, s1i, 0.0), jnp.where(is_key, s2i, 0.0), hi)


def _attn_proj(h, g, w, tabs, ts):
    b, s, d = h.shape
    n = w.shape[1]
    row = lambda i, j: (i, j, 0)
    const = lambda i, j: (0, 0)
    tab_spec = pl.BlockSpec((1, ts, LANES), row)
    return pl.pallas_call(
        _attn_proj_kernel,
        grid=(b, s // ts),
        in_specs=[pl.BlockSpec((1, ts, d), row), pl.BlockSpec((1, d), const),
                  pl.BlockSpec((d, n), const)] + [tab_spec] * 6,
        out_specs=[pl.BlockSpec((1, ts, A_WIDTH), row), pl.BlockSpec((1, ts, LANES), row),
                   pl.BlockSpec((1, ts, LANES), row), pl.BlockSpec((1, ts, IDX_WIDTH), row),
                   pl.BlockSpec((1, ts, LANES), row)],
        out_shape=[jax.ShapeDtypeStruct((b, s, A_WIDTH), BF16),
                   jax.ShapeDtypeStruct((b, s, LANES), BF16),
                   jax.ShapeDtypeStruct((b, s, LANES), BF16),
                   jax.ShapeDtypeStruct((b, s, IDX_WIDTH), BF16),
                   jax.ShapeDtypeStruct((b, s, LANES), F32)],
        compiler_params=_cparams(("arbitrary", "arbitrary")),
        name="attn_proj",
    )(h, g, w, *tabs)


def _sc_proj_kernel(h_ref, g_ref, w_ref, cw_ref, y_ref, carry_ref):
    @pl.when(pl.program_id(1) == 0)
    def _():
        carry_ref[...] = jnp.zeros_like(carry_ref)

    xn = _rms(h_ref[0], g_ref[...])
    ts = xn.shape[0]
    for p in range(SC_WIDTH // LANES):
        lo = p * LANES
        u = _dot(xn, w_ref[:, lo:lo + LANES])
        gate_b = _dot(xn, w_ref[:, SC_WIDTH + lo:SC_WIDTH + lo + LANES])
        gate_c = _dot(xn, w_ref[:, 2 * SC_WIDTH + lo:2 * SC_WIDTH + lo + LANES])
        cu = gate_c * u
        prev = carry_ref[:, lo:lo + LANES]
        conv = (_shift_rows(cu, prev, 2) * cw_ref[0:1, lo:lo + LANES]
                + _shift_rows(cu, prev, 1) * cw_ref[1:2, lo:lo + LANES]
                + cu * cw_ref[2:3, lo:lo + LANES])
        y_ref[0, :, lo:lo + LANES] = (gate_b * conv).astype(BF16)
        carry_ref[:, lo:lo + LANES] = cu[ts - SUBLANES:ts]


def _sc_proj(h, g, w, cw, ts):
    b, s, d = h.shape
    row = lambda i, j: (i, j, 0)
    const = lambda i, j: (0, 0)
    return pl.pallas_call(
        _sc_proj_kernel,
        grid=(b, s // ts),
        in_specs=[pl.BlockSpec((1, ts, d), row), pl.BlockSpec((1, d), const),
                  pl.BlockSpec((d, 3 * SC_WIDTH), const), pl.BlockSpec((3, SC_WIDTH), const)],
        out_specs=pl.BlockSpec((1, ts, SC_WIDTH), row),
        out_shape=jax.ShapeDtypeStruct((b, s, SC_WIDTH), BF16),
        scratch_shapes=[pltpu.VMEM((SUBLANES, SC_WIDTH), F32)],
        compiler_params=_cparams(("arbitrary", "arbitrary")),
        name="sc_proj",
    )(h, g, w, cw)


def _split_dot(x, m):
    hi = x.astype(BF16)
    lo = (x - hi.astype(F32)).astype(BF16)
    return _dot(hi, m) + _dot(lo, m)


def _rw_proj_kernel(h_ref, g_ref, w_ref, mu_ref, wlora_ref, gup_ref, w0_ref, a0_ref, kk_ref,
                    ka_ref, hsum_ref,
                    r_ref, lw_ref, k_ref, v_ref, kn_ref, a_ref, gg_ref, carry_ref):
    @pl.when(pl.program_id(1) == 0)
    def _():
        carry_ref[...] = jnp.zeros_like(carry_ref)

    xn = _rms(h_ref[0], g_ref[...])
    ts = xn.shape[0]

    def mixed(lo, width):
        z = _dot(xn, w_ref[:, lo:lo + width])
        zp = _shift_rows(z, carry_ref[:, lo:lo + width], 1)
        carry_ref[:, lo:lo + width] = z[ts - SUBLANES:ts]
        return z + (zp - z) * mu_ref[:, lo:lo + width]

    lora_lo = 3 * RW_WIDTH
    zl = mixed(lora_lo, LANES)
    lane = lax.broadcasted_iota(jnp.int32, zl.shape, 1)
    zl = jnp.where(lane < RW_W_LORA, jnp.tanh(zl), zl).astype(BF16)
    zg = jax.nn.sigmoid(mixed(lora_lo + LANES, RW_G_LORA)).astype(BF16)
    for p in range(RW_WIDTH // LANES):
        lo = p * LANES
        sl = slice(lo, lo + LANES)
        up = _dot(zl, wlora_ref[:, lo:lo + LANES])
        ua = _dot(zl, wlora_ref[:, RW_WIDTH + lo:RW_WIDTH + lo + LANES])
        lw_ref[0, :, sl] = -RW_DECAY_SCALE * jax.nn.sigmoid(w0_ref[:, sl] + up)
        a = jax.nn.sigmoid(a0_ref[:, sl] + ua)
        a_ref[0, :, sl] = a
        gg_ref[0, :, sl] = _dot(zg, gup_ref[:, sl])
        r_ref[0, :, sl] = mixed(lo, LANES)
        k = mixed(RW_WIDTH + lo, LANES)
        v_ref[0, :, sl] = mixed(2 * RW_WIDTH + lo, LANES)
        kk = k * kk_ref[:, sl]
        ss = _split_dot(kk * kk, hsum_ref[...])
        kn_ref[0, :, sl] = kk * lax.rsqrt(jnp.maximum(ss, 1e-24))
        k_ref[0, :, sl] = k * (1.0 + (a - 1.0) * ka_ref[:, sl])


def _rw_proj(h, g, w, mu, wlora, gup, w0, a0, k_k, k_a, hsum, ts):
    b, s, d = h.shape
    row = lambda i, j: (i, j, 0)
    const = lambda i, j: (0, 0)
    vec = pl.BlockSpec((1, RW_WIDTH), const)
    out_spec = pl.BlockSpec((1, ts, RW_WIDTH), row)
    out_shape = jax.ShapeDtypeStruct((b, s, RW_WIDTH), F32)
    return pl.pallas_call(
        _rw_proj_kernel,
        grid=(b, s // ts),
        in_specs=[pl.BlockSpec((1, ts, d), row), pl.BlockSpec((1, d), const),
                  pl.BlockSpec((d, RW_IN), const), pl.BlockSpec((1, RW_IN), const),
                  pl.BlockSpec((LANES, 2 * RW_WIDTH), const),
                  pl.BlockSpec((RW_G_LORA, RW_WIDTH), const),
                  vec, vec, vec, vec, pl.BlockSpec((LANES, LANES), const)],
        out_specs=[out_spec] * 7,
        out_shape=[out_shape] * 7,
        scratch_shapes=[pltpu.VMEM((SUBLANES, RW_IN), F32)],
        compiler_params=_cparams(("arbitrary", "arbitrary")),
        name="rw_proj",
    )(h, g, w, mu, wlora, gup, w0, a0, k_k, k_a, hsum)


def _gate_proj_kernel(h_ref, g_ref, w_ref, b_ref, o_ref):
    xn = _rms(h_ref[0], g_ref[...])
    n = w_ref.shape[1]
    step = 4 * LANES
    for lo in range(0, n, step):
        y = _dot(xn, w_ref[:, lo:lo + step]) + b_ref[:, lo:lo + step]
        o_ref[0, :, lo:lo + step] = jax.nn.sigmoid(y).astype(BF16)


def _gate_proj(h, g, w, bias, ts):
    b, s, d = h.shape
    n = w.shape[1]
    row = lambda i, j: (i, j, 0)
    const = lambda i, j: (0, 0)
    return pl.pallas_call(
        _gate_proj_kernel,
        grid=(b, s // ts),
        in_specs=[pl.BlockSpec((1, ts, d), row), pl.BlockSpec((1, d), const),
                  pl.BlockSpec((d, n), const), pl.BlockSpec((1, n), const)],
        out_specs=pl.BlockSpec((1, ts, n), row),
        out_shape=jax.ShapeDtypeStruct((b, s, n), BF16),
        compiler_params=_cparams(("arbitrary", "arbitrary")),
        name="gate_proj",
    )(h, g, w, bias)


def _dsa_kernel(q_ref, k_ref, v_ref, qi_ref, kwk_ref, kwq_ref, o_ref, key_ref, bias_ref,
                *, first_block, k_sel):
    tq = q_ref.shape[1]
    sk = k_ref.shape[1]
    t0 = (first_block + pl.program_id(1)) * tq

    ki = kwk_ref[0][:, 0:IDX_DIM].astype(BF16)
    wq = kwq_ref[0]
    qi = qi_ref[0]
    isc = jnp.zeros((tq, sk), F32)
    for h in range(IDX_HEADS):
        rel = jnp.maximum(_dot_nt(qi[:, h * IDX_DIM:(h + 1) * IDX_DIM], ki), 0.0)
        isc = isc + wq[:, IDX_DIM + h:IDX_DIM + h + 1] * rel

    cols = lax.broadcasted_iota(jnp.int32, (tq, sk), 1)
    rows = t0 + lax.broadcasted_iota(jnp.int32, (tq, sk), 0)
    causal = cols <= rows
    bits = pltpu.bitcast(isc, jnp.int32)
    key = bits ^ ((bits >> 31) & jnp.int32(0x7FFFFFFF))
    key_ref[...] = jnp.where(causal, key, INT_MIN)

    def search(it, tau_u):
        bit = jnp.left_shift(jnp.int32(1), 31 - it)
        cand_u = tau_u | bit
        cand_s = cand_u ^ INT_MIN
        cnt = jnp.sum(jnp.where(key_ref[...] >= cand_s, 1.0, 0.0), axis=1, keepdims=True)
        return jnp.where(cnt >= k_sel, cand_u, tau_u)

    tau = lax.fori_loop(0, 32, search, jnp.zeros((tq, 1), jnp.int32)) ^ INT_MIN

    need = k_sel - jnp.sum(jnp.where(key_ref[...] > tau, 1.0, 0.0), axis=1, keepdims=True)
    r2 = lax.broadcasted_iota(jnp.int32, (LANES, 2 * LANES), 0)
    c2 = lax.broadcasted_iota(jnp.int32, (LANES, 2 * LANES), 1)
    tri = jnp.where((r2 < c2) | (c2 >= LANES), 1.0, 0.0).astype(BF16)
    carry = jnp.zeros((tq, LANES), F32)
    rows_j = t0 + lax.broadcasted_iota(jnp.int32, (tq, LANES), 0)
    cols_j = lax.broadcasted_iota(jnp.int32, (tq, LANES), 1)
    for j in range(sk // LANES):
        sl = slice(j * LANES, (j + 1) * LANES)
        key_j = key_ref[:, sl]
        eq_f = jnp.where(key_j == tau, 1.0, 0.0)
        pr = _dot(eq_f.astype(BF16), tri)
        take = jnp.where(carry + pr[:, :LANES] < need, eq_f, 0.0)
        sel = jnp.where(key_j > tau, 1.0, take)
        sel = jnp.where(cols_j + j * LANES <= rows_j, sel, 0.0)
        bias_ref[:, sl] = jnp.where(sel > 0.0, 0.0, -jnp.inf)
        carry = carry + pr[:, LANES:]

    bias = bias_ref[...]
    kk = k_ref[0]
    vv = v_ref[0]
    group = A_HEADS // A_KV_HEADS
    for h in range(A_HEADS):
        n = h // group
        kn = kk[:, n * A_HEAD_DIM:(n + 1) * A_HEAD_DIM]
        vn = vv[:, n * A_HEAD_DIM:(n + 1) * A_HEAD_DIM]
        s = _dot_nt(q_ref[0, :, h * A_HEAD_DIM:(h + 1) * A_HEAD_DIM], kn) + bias
        m = jnp.max(s, axis=1, keepdims=True)
        p = jnp.exp(s - m)
        l = jnp.sum(p, axis=1, keepdims=True)
        o = _dot(p.astype(BF16), vn) / l
        o_ref[0, :, h * A_HEAD_DIM:(h + 1) * A_HEAD_DIM] = o.astype(BF16)


def _dsa(q, k, v, qi, kw, k_sel):
    b, s, _ = q.shape
    tq = min(DSA_Q_TILE, s)
    nblk = s // tq
    per = max(1, nblk // DSA_CLASSES)
    outs = []
    for first in range(0, nblk, per):
        sk = (first + per) * tq
        qmap = lambda i, j, first=first: (i, first + j, 0)
        kmap = lambda i, j: (i, 0, 0)
        outs.append(pl.pallas_call(
            functools.partial(_dsa_kernel, first_block=first, k_sel=k_sel),
            grid=(b, per),
            in_specs=[pl.BlockSpec((1, tq, A_WIDTH), qmap), pl.BlockSpec((1, sk, LANES), kmap),
                      pl.BlockSpec((1, sk, LANES), kmap), pl.BlockSpec((1, tq, IDX_WIDTH), qmap),
                      pl.BlockSpec((1, sk, LANES), kmap), pl.BlockSpec((1, tq, LANES), qmap)],
            out_specs=pl.BlockSpec((1, tq, A_WIDTH), lambda i, j: (i, j, 0)),
            out_shape=jax.ShapeDtypeStruct((b, per * tq, A_WIDTH), BF16),
            scratch_shapes=[pltpu.VMEM((tq, sk), jnp.int32), pltpu.VMEM((tq, sk), F32)],
            compiler_params=_cparams(("arbitrary", "arbitrary")),
            name=f"dsa_{sk}",
        )(q, k, v, qi, kw, kw))
    return jnp.concatenate(outs, axis=1)


def _rwkv_kernel(r_ref, lw_ref, k_ref, v_ref, kn_ref, a_ref, g_ref, rk_ref, lnw_ref, lnb_ref,
                 y_ref, st_ref):
    @pl.when(pl.program_id(1) == 0)
    def _():
        st_ref[...] = jnp.zeros_like(st_ref)

    r, lw, kf, v = r_ref[0], lw_ref[0], k_ref[0], v_ref[0]
    kn, a, g = kn_ref[0], a_ref[0], g_ref[0]
    c, width = r.shape
    hd = RW_HEAD_DIM

    rows = lax.broadcasted_iota(jnp.int32, (c, width), 0)
    cum = lw
    step = 1
    while step < c:
        cum = cum + jnp.where(rows >= step, pltpu.roll(cum, step, 0), 0.0)
        step *= 2
    cum_last = cum[c - 1:c, :]
    p_inc = jnp.exp(cum)
    p_exc = jnp.exp(cum - lw)
    p_inv = jnp.exp(-cum)
    p_rem = jnp.exp(cum_last - cum)
    p_last = jnp.exp(cum_last)
    kb = kn * a
    a_t = -kn * p_exc
    r_t = r * p_inc
    b_t = kb * p_inv
    k_t = kf * p_inv
    b_h = kb * p_rem
    k_h = kf * p_rem
    bonus_in = r * kf * rk_ref[...]

    lane = lax.broadcasted_iota(jnp.int32, (c, LANES), 1)
    first = lane < hd
    r2 = lax.broadcasted_iota(jnp.int32, (2 * c, 2 * c), 0)
    c2 = lax.broadcasted_iota(jnp.int32, (2 * c, 2 * c), 1)
    same_head = (r2 >= c) == (c2 >= c)
    t_row = r2 & (c - 1)
    t_col = c2 & (c - 1)
    strict = same_head & (t_col < t_row)
    lower = same_head & (t_col <= t_row)

    def blk(x):
        return jnp.concatenate([jnp.where(first, x, 0.0), jnp.where(first, 0.0, x)],
                               axis=0).astype(BF16)

    def head_sum(x):
        s0 = jnp.sum(jnp.where(first, x, 0.0), axis=1, keepdims=True)
        s1 = jnp.sum(jnp.where(first, 0.0, x), axis=1, keepdims=True)
        return jnp.where(first, s0, s1)

    pairs = range(width // LANES)
    sls = [slice(p * LANES, (p + 1) * LANES) for p in pairs]
    n = 2 * c
    st = [st_ref[p] for p in pairs]
    st_b = [s.astype(BF16) for s in st]
    la = [blk(a_t[:, sl]) for sl in sls]
    lr = [blk(r_t[:, sl]) for sl in sls]
    vb = [blk(v[:, sl]) for sl in sls]
    gram = [_dot_nt(jnp.concatenate([la[p], lr[p]], axis=0),
                    jnp.concatenate([blk(b_t[:, sls[p]]), blk(k_t[:, sls[p]])], axis=0))
            for p in pairs]
    a_ab = [jnp.where(strict, gm[:n, :n], 0.0) for gm in gram]
    x = list(a_ab)
    npow = list(a_ab)
    for _ in range(int(math.log2(c)) - 1):
        npow_b = [m.astype(BF16) for m in npow]
        npow = [_dot(m, m) for m in npow_b]
        x = [x[p] + npow[p] + _dot(x[p].astype(BF16), npow[p].astype(BF16)) for p in pairs]
    rhs = [_dot_nt(la[p], st_b[p])
           + _dot(jnp.where(strict, gram[p][:n, n:], 0.0).astype(BF16), vb[p]) for p in pairs]
    sa = [rhs[p] + _dot(x[p].astype(BF16), rhs[p].astype(BF16)) for p in pairs]
    sav = [jnp.concatenate([sa[p].astype(BF16), vb[p]], axis=0) for p in pairs]
    yb = [_dot_nt(lr[p], st_b[p])
          + _dot(jnp.where(jnp.concatenate([lower, lower], axis=1), gram[p][n:, :], 0.0).astype(BF16),
                 sav[p]) for p in pairs]
    for p in pairs:
        sl = sls[p]
        st_ref[p] = st[p] * p_last[:, sl] + _dot_tn(
            sav[p], jnp.concatenate([blk(b_h[:, sl]), blk(k_h[:, sl])], axis=0))
    for p in pairs:
        sl = sls[p]
        y = yb[p][:c] + yb[p][c:]
        mean = head_sum(y) * (1.0 / hd)
        yc = y - mean
        var = head_sum(yc * yc) * (1.0 / hd)
        yn = yc * lax.rsqrt(var + RW_GN_EPS) * lnw_ref[:, sl] + lnb_ref[:, sl]
        bonus = head_sum(bonus_in[:, sl]) * v[:, sl]
        y_ref[0, :, sl] = ((yn + bonus) * g[:, sl]).astype(BF16)


def _rwkv(r, lw, kf, v, kn, a, g, r_k, ln_w, ln_b):
    b, s, w = r.shape
    c = min(RW_CHUNK, s)
    row = lambda i, j: (i, j, 0)
    const = lambda i, j: (0, 0)
    seq = pl.BlockSpec((1, c, w), row)
    vec = pl.BlockSpec((1, w), const)
    return pl.pallas_call(
        _rwkv_kernel,
        grid=(b, s // c),
        in_specs=[seq] * 7 + [vec] * 3,
        out_specs=seq,
        out_shape=jax.ShapeDtypeStruct((b, s, w), BF16),
        scratch_shapes=[pltpu.VMEM((w // LANES, LANES, LANES), F32)],
        compiler_params=_cparams(("arbitrary", "arbitrary")),
        name="rwkv7",
    )(r, lw, kf, v, kn, a, g, r_k, ln_w, ln_b)


def _merge_kernel(h_ref, ya_ref, yb_ref, yc_ref, gt_ref, wb_ref, wo_ref, o_ref):
    d = h_ref.shape[2]
    mixed = None
    for i, y_ref in enumerate((ya_ref, yb_ref, yc_ref)):
        up = _dot(y_ref[0], wb_ref[i])
        term = gt_ref[0, :, i * d:(i + 1) * d].astype(F32) * up
        mixed = term if mixed is None else mixed + term
    o_ref[0] = h_ref[0] + _dot(mixed.astype(BF16), wo_ref[...])


def _merge(h, ya, yb, yc, gates, wb, wo, ts):
    b, s, d = h.shape
    row = lambda i, j: (i, j, 0)
    bw = ya.shape[2]
    return pl.pallas_call(
        _merge_kernel,
        grid=(b, s // ts),
        in_specs=[pl.BlockSpec((1, ts, d), row)] + [pl.BlockSpec((1, ts, bw), row)] * 3
                 + [pl.BlockSpec((1, ts, N_BRANCH * d), row),
                    pl.BlockSpec((N_BRANCH, bw, d), lambda i, j: (0, 0, 0)),
                    pl.BlockSpec((d, d), lambda i, j: (0, 0))],
        out_specs=pl.BlockSpec((1, ts, d), row),
        out_shape=jax.ShapeDtypeStruct((b, s, d), F32),
        compiler_params=_cparams(("arbitrary", "arbitrary")),
        name="merge",
    )(h, ya, yb, yc, gates, wb, wo)


def _ffn_kernel(h_ref, g_ref, wg_ref, wu_ref, cg_ref, cu_ref, wd_ref, o_ref,
                xn_ref, acc_ref, act_ref, carry_g_ref, carry_u_ref):
    s_idx = pl.program_id(1)
    f = pl.program_id(2)
    ts = h_ref.shape[1]
    tf = wg_ref.shape[1]

    @pl.when(f == 0)
    def _():
        xn_ref[...] = _rms(h_ref[0], g_ref[...])
        acc_ref[...] = jnp.zeros_like(acc_ref)

    @pl.when(s_idx == 0)
    def _():
        carry_g_ref[f] = jnp.zeros(carry_g_ref.shape[1:], F32)
        carry_u_ref[f] = jnp.zeros(carry_u_ref.shape[1:], F32)

    xn = xn_ref[...]

    def conv(w_ref, c_ref, carry_ref, sl):
        u = _dot(xn, w_ref[:, sl])
        prev = carry_ref[f, :, sl]
        out = (_shift_rows(u, prev, 2) * c_ref[0:1, sl] + _shift_rows(u, prev, 1) * c_ref[1:2, sl]
               + u * c_ref[2:3, sl])
        carry_ref[f, :, sl] = u[ts - SUBLANES:ts]
        return out

    step = 2 * LANES
    for lo in range(0, tf, step):
        sl = slice(lo, min(lo + step, tf))
        gate = conv(wg_ref, cg_ref, carry_g_ref, sl)
        up = conv(wu_ref, cu_ref, carry_u_ref, sl)
        act_ref[:, sl] = (gate * jax.nn.sigmoid(gate) * up).astype(BF16)
    acc_ref[...] += _dot(act_ref[...], wd_ref[...])

    @pl.when(f == pl.num_programs(2) - 1)
    def _():
        o_ref[0] = h_ref[0] + acc_ref[...]


def _ffn(h, g, w_up, w_conv, w_down, ts, tf):
    b, s, d = h.shape
    dff = w_down.shape[0]
    nf = dff // tf
    row = lambda i, j, f: (i, j, 0)
    return pl.pallas_call(
        _ffn_kernel,
        grid=(b, s // ts, nf),
        in_specs=[pl.BlockSpec((1, ts, d), row), pl.BlockSpec((1, d), lambda i, j, f: (0, 0)),
                  pl.BlockSpec((d, tf), lambda i, j, f: (0, f)),
                  pl.BlockSpec((d, tf), lambda i, j, f: (0, nf + f)),
                  pl.BlockSpec((3, tf), lambda i, j, f: (0, f)),
                  pl.BlockSpec((3, tf), lambda i, j, f: (0, nf + f)),
                  pl.BlockSpec((tf, d), lambda i, j, f: (f, 0))],
        out_specs=pl.BlockSpec((1, ts, d), row),
        out_shape=jax.ShapeDtypeStruct((b, s, d), F32),
        scratch_shapes=[pltpu.VMEM((ts, d), BF16), pltpu.VMEM((ts, d), F32),
                        pltpu.VMEM((ts, tf), BF16),
                        pltpu.VMEM((nf, SUBLANES, tf), F32), pltpu.VMEM((nf, SUBLANES, tf), F32)],
        compiler_params=_cparams(("arbitrary", "arbitrary", "arbitrary")),
        name="conv_glu_ffn",
    )(h, g, w_up, w_up, w_conv, w_conv, w_down)


def _final_norm_kernel(h_ref, g_ref, o_ref):
    x = h_ref[0]
    ms = jnp.mean(x * x, axis=-1, keepdims=True)
    o_ref[0] = x * lax.rsqrt(ms + RMS_EPS) * g_ref[...]


def _final_norm(h, g, ts):
    b, s, d = h.shape
    row = lambda i, j: (i, j, 0)
    return pl.pallas_call(
        _final_norm_kernel,
        grid=(b, s // ts),
        in_specs=[pl.BlockSpec((1, ts, d), row), pl.BlockSpec((1, d), lambda i, j: (0, 0))],
        out_specs=pl.BlockSpec((1, ts, d), row),
        out_shape=jax.ShapeDtypeStruct((b, s, d), F32),
        compiler_params=_cparams(("arbitrary", "arbitrary")),
        name="final_norm",
    )(h, g)


def _rope_tables(positions, head_dim):
    rot = head_dim // ROPE_FRACTION
    half = rot // 2
    inv = ROPE_THETA ** (-jnp.arange(half, dtype=F32) * 2.0 / rot)
    ang = positions.astype(F32)[:, :, None] * inv
    cos, sin = jnp.cos(ang), jnp.sin(ang)
    zeros = jnp.zeros_like(cos)
    rest = head_dim - rot
    pad1 = jnp.ones(cos.shape[:2] + (rest,), F32)
    pad0 = jnp.zeros(cos.shape[:2] + (rest,), F32)
    reps = LANES // head_dim
    c = jnp.tile(jnp.concatenate([cos, cos, pad1], axis=-1), (1, 1, reps))
    s1 = jnp.tile(jnp.concatenate([-sin, zeros, pad0], axis=-1), (1, 1, reps))
    s2 = jnp.tile(jnp.concatenate([zeros, sin, pad0], axis=-1), (1, 1, reps))
    return c, s1, s2


def kernel(x, positions, norm_mix, w_in, b_gate, sc_conv, rw_mu, rw_w0, rw_w_up, rw_a0, rw_a_up, rw_g_up, rw_k_k, rw_k_a, rw_r_k, rw_ln_w, rw_ln_b, w_branch, w_out, norm_ffn, ffn_up, ffn_conv, ffn_down, norm_final):
    bsz, seq, d = x.shape
    depth = w_in.shape[0]
    ts = min(ROW_TILE, seq)
    k_sel = min(TOPK_MAX, seq // 4)

    tabs = _rope_tables(positions, A_HEAD_DIM) + _rope_tables(positions, IDX_DIM)
    o_q = 0
    o_k = o_q + A_WIDTH
    o_v = o_k + A_KV_WIDTH
    o_qi = o_v + A_KV_WIDTH
    o_ki = o_qi + IDX_WIDTH
    o_wi = o_ki + IDX_DIM
    o_sc = o_wi + IDX_HEADS
    o_rw = o_sc + 3 * SC_WIDTH
    o_gate = o_rw + RW_IN
    attn_scale = A_HEAD_DIM ** -0.5
    idx_scale = (IDX_HEADS ** -0.5) * (IDX_DIM ** -0.5)
    head_sum = jnp.kron(jnp.eye(LANES // RW_HEAD_DIM, dtype=F32),
                        jnp.ones((RW_HEAD_DIM, RW_HEAD_DIM), F32)).astype(BF16)

    h = x
    for l in range(depth):
        w = w_in[l]
        pad = jnp.zeros((d, LANES - IDX_DIM - IDX_HEADS), F32)
        w_attn = jnp.concatenate(
            [w[:, o_q:o_k] * attn_scale, w[:, o_k:o_ki], w[:, o_ki:o_wi],
             w[:, o_wi:o_sc] * idx_scale, pad], axis=1).astype(BF16)
        w_sc = w[:, o_sc:o_rw].astype(BF16)
        w_rw = w[:, o_rw:o_gate].astype(BF16)
        w_gate = w[:, o_gate:].astype(BF16)
        g_mix = norm_mix[l][None, :]

        q, k, v, qi, kw = _attn_proj(h, g_mix, w_attn, tabs, ts)
        y_b = _sc_proj(h, g_mix, w_sc, sc_conv[l], ts)
        zero_lora = jnp.zeros((RW_W_LORA, RW_WIDTH), F32)
        w_lora = jnp.concatenate(
            [jnp.concatenate([rw_w_up[l], zero_lora], axis=1),
             jnp.concatenate([zero_lora, rw_a_up[l]], axis=1)], axis=0).astype(BF16)
        r, lw, kf, vv, kn, a, gg = _rw_proj(
            h, g_mix, w_rw, rw_mu[l][None, :], w_lora, rw_g_up[l].astype(BF16),
            rw_w0[l][None, :], rw_a0[l][None, :], rw_k_k[l][None, :], rw_k_a[l][None, :],
            head_sum, ts)
        gates = _gate_proj(h, g_mix, w_gate, b_gate[l][None, :], ts)

        y_a = _dsa(q, k, v, qi, kw, k_sel)
        y_c = _rwkv(r, lw, kf, vv, kn, a, gg, rw_r_k[l].reshape(1, RW_WIDTH),
                    rw_ln_w[l][None, :], rw_ln_b[l][None, :])

        h = _merge(h, y_a, y_b, y_c, gates, w_branch[l].astype(BF16), w_out[l].astype(BF16), ts)
        h = _ffn(h, norm_ffn[l][None, :], ffn_up[l].astype(BF16), ffn_conv[l],
                 ffn_down[l].astype(BF16), ts, FF_TILE)
    return _final_norm(h, norm_final[None, :], ts)
```

```python
import functools
import math

import jax
import jax.numpy as jnp
from jax import lax
from jax.experimental import pallas as pl
from jax.experimental.pallas import tpu as pltpu

F32 = jnp.float32
BF16 = jnp.bfloat16
HIGHEST = lax.Precision.HIGHEST

D_MODEL = 1024
A_HEADS = 8
A_KV_HEADS = 2
A_HEAD_DIM = 64
A_WIDTH = A_HEADS * A_HEAD_DIM
A_KV_WIDTH = A_KV_HEADS * A_HEAD_DIM
IDX_HEADS = 8
IDX_DIM = 32
IDX_WIDTH = IDX_HEADS * IDX_DIM
TOPK_MAX = 256
ROPE_THETA = 500000.0
ROPE_FRACTION = 4
SC_WIDTH = 512
RW_HEADS = 8
RW_HEAD_DIM = 64
RW_WIDTH = RW_HEADS * RW_HEAD_DIM
RW_W_LORA = 64
RW_A_LORA = 64
RW_G_LORA = 128
RW_IN = 3 * RW_WIDTH + RW_W_LORA + RW_A_LORA + RW_G_LORA
RW_GN_EPS = 64e-5
RW_DECAY_SCALE = math.exp(-0.5)
N_BRANCH = 3
D_FF = 2816
RMS_EPS = 1e-6

LANES = 128
MXU_COLS = 256
SUBLANES = 8
INT_MIN = -2147483648
VMEM_LIMIT = 48 * 1024 * 1024

ROW_TILE = 512
DSA_Q_TILE = 256
DSA_CLASSES = 4
RW_CHUNK = 64
RW_CHUNKS_PER_STEP = 4


def _cparams(sem):
    return pltpu.CompilerParams(dimension_semantics=sem, vmem_limit_bytes=VMEM_LIMIT)


def _rms(x, g):
    ms = jnp.mean(x * x, axis=-1, keepdims=True)
    return (x * lax.rsqrt(ms + RMS_EPS) * g).astype(BF16)


def _dot(a, b, precision=None):
    return jnp.dot(a, b, preferred_element_type=F32, precision=precision)


def _dot_nt(a, b, precision=None):
    return lax.dot_general(a, b, (((1,), (1,)), ((), ())), preferred_element_type=F32,
                           precision=precision)


def _dot_tn(a, b, precision=None):
    return lax.dot_general(a, b, (((0,), (0,)), ((), ())), preferred_element_type=F32,
                           precision=precision)


def _shift_rows(x, prev, k):
    xr = pltpu.roll(x, k, 0)
    pr = pltpu.roll(prev, k, 0)
    rows = lax.broadcasted_iota(jnp.int32, pr.shape, 0)
    head = jnp.where(rows < k, pr, xr[0:SUBLANES])
    return jnp.concatenate([head, xr[SUBLANES:]], axis=0)


def _rope_slab(y, c, s1, s2, half):
    return y * c + pltpu.roll(y, LANES - half, 1) * s1 + pltpu.roll(y, half, 1) * s2


def _attn_proj_kernel(h_ref, g_ref, w_ref, cq_ref, s1q_ref, s2q_ref, ci_ref, s1i_ref, s2i_ref,
                      q_ref, k_ref, v_ref, qi_ref, kw_ref):
    xn = _rms(h_ref[0], g_ref[...])
    cq, s1q, s2q = cq_ref[0], s1q_ref[0], s2q_ref[0]
    ci, s1i, s2i = ci_ref[0], s1i_ref[0], s2i_ref[0]
    hq = A_HEAD_DIM // ROPE_FRACTION // 2
    hi = IDX_DIM // ROPE_FRACTION // 2
    col = 0
    for p in range(A_WIDTH // MXU_COLS):
        y = _dot(xn, w_ref[:, col:col + MXU_COLS])
        for half in range(MXU_COLS // LANES):
            lo = p * MXU_COLS + half * LANES
            q_ref[0, :, lo:lo + LANES] = _rope_slab(
                y[:, half * LANES:(half + 1) * LANES], cq, s1q, s2q, hq).astype(BF16)
        col += MXU_COLS
    y = _dot(xn, w_ref[:, col:col + 2 * LANES])
    k_ref[0] = _rope_slab(y[:, :LANES], cq, s1q, s2q, hq).astype(BF16)
    v_ref[0] = y[:, LANES:].astype(BF16)
    col += 2 * LANES
    y = _dot(xn, w_ref[:, col:col + IDX_WIDTH])
    for half in range(IDX_WIDTH // LANES):
        qi_ref[0, :, half * LANES:(half + 1) * LANES] = _rope_slab(
            y[:, half * LANES:(half + 1) * LANES], ci, s1i, s2i, hi).astype(BF16)
    col += IDX_WIDTH
    y = _dot(xn, w_ref[:, col:col + LANES])
    lane = lax.broadcasted_iota(jnp.int32, y.shape, 1)
    is_key = lane < IDX_DIM
    ck = jnp.where(is_key, ci, 1.0)
    kw_ref[0] = _rope_slab(y, ck, jnp.where(is_key, s1i, 0.0), jnp.where(is_key, s2i, 0.0), hi)


def _attn_proj(h, g, w, tabs, ts):
    b, s, d = h.shape
    n = w.shape[1]
    row = lambda i, j: (i, j, 0)
    const = lambda i, j: (0, 0)
    tab_spec = pl.BlockSpec((1, ts, LANES), row)
    return pl.pallas_call(
        _attn_proj_kernel,
        grid=(b, s // ts),
        in_specs=[pl.BlockSpec((1, ts, d), row), pl.BlockSpec((1, d), const),
                  pl.BlockSpec((d, n), const)] + [tab_spec] * 6,
        out_specs=[pl.BlockSpec((1, ts, A_WIDTH), row), pl.BlockSpec((1, ts, LANES), row),
                   pl.BlockSpec((1, ts, LANES), row), pl.BlockSpec((1, ts, IDX_WIDTH), row),
                   pl.BlockSpec((1, ts, LANES), row)],
        out_shape=[jax.ShapeDtypeStruct((b, s, A_WIDTH), BF16),
                   jax.ShapeDtypeStruct((b, s, LANES), BF16),
                   jax.ShapeDtypeStruct((b, s, LANES), BF16),
                   jax.ShapeDtypeStruct((b, s, IDX_WIDTH), BF16),
                   jax.ShapeDtypeStruct((b, s, LANES), F32)],
        compiler_params=_cparams(("arbitrary", "arbitrary")),
        name="attn_proj",
    )(h, g, w, *tabs)


def _sc_proj_kernel(h_ref, g_ref, w_ref, cw_ref, y_ref, carry_ref):
    @pl.when(pl.program_id(1) == 0)
    def _():
        carry_ref[...] = jnp.zeros_like(carry_ref)

    xn = _rms(h_ref[0], g_ref[...])
    ts = xn.shape[0]
    wd = MXU_COLS
    for lo in range(0, SC_WIDTH, wd):
        u = _dot(xn, w_ref[:, lo:lo + wd])
        gate_b = _dot(xn, w_ref[:, SC_WIDTH + lo:SC_WIDTH + lo + wd])
        gate_c = _dot(xn, w_ref[:, 2 * SC_WIDTH + lo:2 * SC_WIDTH + lo + wd])
        cu = gate_c * u
        prev = carry_ref[:, lo:lo + wd]
        conv = (_shift_rows(cu, prev, 2) * cw_ref[0:1, lo:lo + wd]
                + _shift_rows(cu, prev, 1) * cw_ref[1:2, lo:lo + wd]
                + cu * cw_ref[2:3, lo:lo + wd])
        y_ref[0, :, lo:lo + wd] = (gate_b * conv).astype(BF16)
        carry_ref[:, lo:lo + wd] = cu[ts - SUBLANES:ts]


def _sc_proj(h, g, w, cw, ts):
    b, s, d = h.shape
    row = lambda i, j: (i, j, 0)
    const = lambda i, j: (0, 0)
    return pl.pallas_call(
        _sc_proj_kernel,
        grid=(b, s // ts),
        in_specs=[pl.BlockSpec((1, ts, d), row), pl.BlockSpec((1, d), const),
                  pl.BlockSpec((d, 3 * SC_WIDTH), const), pl.BlockSpec((3, SC_WIDTH), const)],
        out_specs=pl.BlockSpec((1, ts, SC_WIDTH), row),
        out_shape=jax.ShapeDtypeStruct((b, s, SC_WIDTH), BF16),
        scratch_shapes=[pltpu.VMEM((SUBLANES, SC_WIDTH), F32)],
        compiler_params=_cparams(("arbitrary", "arbitrary")),
        name="sc_proj",
    )(h, g, w, cw)


def _split_dot(x, m):
    hi = x.astype(BF16)
    lo = (x - hi.astype(F32)).astype(BF16)
    return _dot(hi, m) + _dot(lo, m)


def _rw_proj_kernel(h_ref, g_ref, w_ref, mu_ref, wlora_ref, gup_ref, w0_ref, a0_ref, kk_ref,
                    ka_ref, hsum_ref,
                    r_ref, lw_ref, k_ref, v_ref, kn_ref, a_ref, gg_ref, carry_ref):
    @pl.when(pl.program_id(1) == 0)
    def _():
        carry_ref[...] = jnp.zeros_like(carry_ref)

    xn = _rms(h_ref[0], g_ref[...])
    ts = xn.shape[0]

    def mixed(lo, width):
        z = _dot(xn, w_ref[:, lo:lo + width])
        zp = _shift_rows(z, carry_ref[:, lo:lo + width], 1)
        carry_ref[:, lo:lo + width] = z[ts - SUBLANES:ts]
        return z + (zp - z) * mu_ref[:, lo:lo + width]

    lora_lo = 3 * RW_WIDTH
    zlg = mixed(lora_lo, RW_IN - lora_lo)
    zl = zlg[:, :LANES]
    lane = lax.broadcasted_iota(jnp.int32, zl.shape, 1)
    zl = jnp.where(lane < RW_W_LORA, jnp.tanh(zl), zl).astype(BF16)
    zg = jax.nn.sigmoid(zlg[:, LANES:]).astype(BF16)
    wd = MXU_COLS
    for lo in range(0, RW_WIDTH, wd):
        sl = slice(lo, lo + wd)
        up = _dot(zl, wlora_ref[:, lo:lo + wd])
        ua = _dot(zl, wlora_ref[:, RW_WIDTH + lo:RW_WIDTH + lo + wd])
        lw_ref[0, :, sl] = -RW_DECAY_SCALE * jax.nn.sigmoid(w0_ref[:, sl] + up)
        a = jax.nn.sigmoid(a0_ref[:, sl] + ua)
        a_ref[0, :, sl] = a
        gg_ref[0, :, sl] = _dot(zg, gup_ref[:, sl])
        r_ref[0, :, sl] = mixed(lo, wd)
        k = mixed(RW_WIDTH + lo, wd)
        v_ref[0, :, sl] = mixed(2 * RW_WIDTH + lo, wd)
        kk = k * kk_ref[:, sl]
        ss = _split_dot(kk * kk, hsum_ref[...])
        kn_ref[0, :, sl] = kk * lax.rsqrt(jnp.maximum(ss, 1e-24))
        k_ref[0, :, sl] = k * (1.0 + (a - 1.0) * ka_ref[:, sl])


def _rw_proj(h, g, w, mu, wlora, gup, w0, a0, k_k, k_a, hsum, ts):
    b, s, d = h.shape
    row = lambda i, j: (i, j, 0)
    const = lambda i, j: (0, 0)
    vec = pl.BlockSpec((1, RW_WIDTH), const)
    out_spec = pl.BlockSpec((1, ts, RW_WIDTH), row)
    out_shape = jax.ShapeDtypeStruct((b, s, RW_WIDTH), F32)
    return pl.pallas_call(
        _rw_proj_kernel,
        grid=(b, s // ts),
        in_specs=[pl.BlockSpec((1, ts, d), row), pl.BlockSpec((1, d), const),
                  pl.BlockSpec((d, RW_IN), const), pl.BlockSpec((1, RW_IN), const),
                  pl.BlockSpec((LANES, 2 * RW_WIDTH), const),
                  pl.BlockSpec((RW_G_LORA, RW_WIDTH), const),
                  vec, vec, vec, vec, pl.BlockSpec((MXU_COLS, MXU_COLS), const)],
        out_specs=[out_spec] * 7,
        out_shape=[out_shape] * 7,
        scratch_shapes=[pltpu.VMEM((SUBLANES, RW_IN), F32)],
        compiler_params=_cparams(("arbitrary", "arbitrary")),
        name="rw_proj",
    )(h, g, w, mu, wlora, gup, w0, a0, k_k, k_a, hsum)


def _gate_proj_kernel(h_ref, g_ref, w_ref, b_ref, o_ref):
    xn = _rms(h_ref[0], g_ref[...])
    n = w_ref.shape[1]
    step = 4 * LANES
    for lo in range(0, n, step):
        y = _dot(xn, w_ref[:, lo:lo + step]) + b_ref[:, lo:lo + step]
        o_ref[0, :, lo:lo + step] = jax.nn.sigmoid(y).astype(BF16)


def _gate_proj(h, g, w, bias, ts):
    b, s, d = h.shape
    n = w.shape[1]
    row = lambda i, j: (i, j, 0)
    const = lambda i, j: (0, 0)
    return pl.pallas_call(
        _gate_proj_kernel,
        grid=(b, s // ts),
        in_specs=[pl.BlockSpec((1, ts, d), row), pl.BlockSpec((1, d), const),
                  pl.BlockSpec((d, n), const), pl.BlockSpec((1, n), const)],
        out_specs=pl.BlockSpec((1, ts, n), row),
        out_shape=jax.ShapeDtypeStruct((b, s, n), BF16),
        compiler_params=_cparams(("arbitrary", "arbitrary")),
        name="gate_proj",
    )(h, g, w, bias)


def _dsa_kernel(q_ref, k_ref, v_ref, qi_ref, kwk_ref, kwq_ref, o_ref, key_ref, half_ref, bias_ref,
                *, first_block, k_sel):
    tq = q_ref.shape[1]
    sk = k_ref.shape[1]
    t0 = (first_block + pl.program_id(1)) * tq

    ki = kwk_ref[0][:, 0:IDX_DIM].astype(BF16)
    wq = kwq_ref[0]
    qi = qi_ref[0]
    isc = jnp.zeros((tq, sk), F32)
    for h in range(IDX_HEADS):
        rel = jnp.maximum(_dot_nt(qi[:, h * IDX_DIM:(h + 1) * IDX_DIM], ki), 0.0)
        isc = isc + wq[:, IDX_DIM + h:IDX_DIM + h + 1] * rel

    cols = lax.broadcasted_iota(jnp.int32, (tq, sk), 1)
    rows = t0 + lax.broadcasted_iota(jnp.int32, (tq, sk), 0)
    causal = cols <= rows
    bits = pltpu.bitcast(isc, jnp.int32)
    key = bits ^ ((bits >> 31) & jnp.int32(0x7FFFFFFF))
    key = jnp.where(causal, key, INT_MIN)
    key_ref[...] = key

    half_min = -(1 << 15)

    def search_half(src_ref):
        def body(it, tau_u):
            cand_u = tau_u | jnp.left_shift(jnp.int32(1), 15 - it)
            cand = jnp.broadcast_to(cand_u + half_min, (tq, LANES)).astype(jnp.int16)
            accs = [jnp.zeros((tq, LANES), jnp.int16) for _ in range(4)]
            for j in range(sk // LANES):
                hit = src_ref[:, j * LANES:(j + 1) * LANES] >= cand
                accs[j % 4] = accs[j % 4] + jnp.where(hit, jnp.int16(1), jnp.int16(0))
            acc = (accs[0] + accs[1]) + (accs[2] + accs[3])
            cnt = jnp.sum(acc.astype(F32), axis=1, keepdims=True)
            return jnp.where(cnt >= k_sel, cand_u, tau_u)
        return lax.fori_loop(0, 16, body, jnp.zeros((tq, 1), jnp.int32), unroll=True)

    half_ref[...] = (key >> 16).astype(jnp.int16)
    tau_hi = search_half(half_ref) + half_min
    key = key_ref[...]
    hi = key >> 16
    lo = (key & jnp.int32(0xFFFF)) + half_min
    half_ref[...] = jnp.where(hi > tau_hi, -half_min - 1,
                              jnp.where(hi == tau_hi, lo, half_min)).astype(jnp.int16)
    tau = (tau_hi << 16) | search_half(half_ref)

    need = k_sel - jnp.sum(jnp.where(key_ref[...] > tau, 1.0, 0.0), axis=1, keepdims=True)
    r2 = lax.broadcasted_iota(jnp.int32, (LANES, 2 * LANES), 0)
    c2 = lax.broadcasted_iota(jnp.int32, (LANES, 2 * LANES), 1)
    tri = jnp.where((r2 < c2) | (c2 >= LANES), 1.0, 0.0).astype(BF16)
    carry = jnp.zeros((tq, LANES), F32)
    rows_j = t0 + lax.broadcasted_iota(jnp.int32, (tq, LANES), 0)
    cols_j = lax.broadcasted_iota(jnp.int32, (tq, LANES), 1)
    for j in range(sk // LANES):
        sl = slice(j * LANES, (j + 1) * LANES)
        key_j = key_ref[:, sl]
        eq_f = jnp.where(key_j == tau, 1.0, 0.0)
        pr = _dot(eq_f.astype(BF16), tri)
        take = jnp.where(carry + pr[:, :LANES] < need, eq_f, 0.0)
        sel = jnp.where(key_j > tau, 1.0, take)
        sel = jnp.where(cols_j + j * LANES <= rows_j, sel, 0.0)
        bias_ref[:, sl] = jnp.where(sel > 0.0, 0.0, -jnp.inf)
        carry = carry + pr[:, LANES:]

    bias = bias_ref[...]
    kk = k_ref[0]
    vv = v_ref[0]
    group = A_HEADS // A_KV_HEADS
    for h in range(A_HEADS):
        n = h // group
        kn = kk[:, n * A_HEAD_DIM:(n + 1) * A_HEAD_DIM]
        vn = vv[:, n * A_HEAD_DIM:(n + 1) * A_HEAD_DIM]
        s = _dot_nt(q_ref[0, :, h * A_HEAD_DIM:(h + 1) * A_HEAD_DIM], kn) + bias
        m = jnp.max(s, axis=1, keepdims=True)
        p = jnp.exp(s - m)
        l = jnp.sum(p, axis=1, keepdims=True)
        o = _dot(p.astype(BF16), vn) / l
        o_ref[0, :, h * A_HEAD_DIM:(h + 1) * A_HEAD_DIM] = o.astype(BF16)


def _dsa(q, k, v, qi, kw, k_sel):
    b, s, _ = q.shape
    tq = min(DSA_Q_TILE, s)
    nblk = s // tq
    per = max(1, nblk // DSA_CLASSES)
    outs = []
    for first in range(0, nblk, per):
        sk = (first + per) * tq
        qmap = lambda i, j, first=first: (i, first + j, 0)
        kmap = lambda i, j: (i, 0, 0)
        outs.append(pl.pallas_call(
            functools.partial(_dsa_kernel, first_block=first, k_sel=k_sel),
            grid=(b, per),
            in_specs=[pl.BlockSpec((1, tq, A_WIDTH), qmap), pl.BlockSpec((1, sk, LANES), kmap),
                      pl.BlockSpec((1, sk, LANES), kmap), pl.BlockSpec((1, tq, IDX_WIDTH), qmap),
                      pl.BlockSpec((1, sk, LANES), kmap), pl.BlockSpec((1, tq, LANES), qmap)],
            out_specs=pl.BlockSpec((1, tq, A_WIDTH), lambda i, j: (i, j, 0)),
            out_shape=jax.ShapeDtypeStruct((b, per * tq, A_WIDTH), BF16),
            scratch_shapes=[pltpu.VMEM((tq, sk), jnp.int32), pltpu.VMEM((tq, sk), jnp.int16),
                            pltpu.VMEM((tq, sk), F32)],
            compiler_params=_cparams(("arbitrary", "arbitrary")),
            name=f"dsa_{sk}",
        )(q, k, v, qi, kw, kw))
    return jnp.concatenate(outs, axis=1)


def _rwkv_kernel(r_ref, lw_ref, k_ref, v_ref, kn_ref, a_ref, g_ref, rk_ref, lnw_ref, lnb_ref,
                 y_ref, st_ref, *, c):
    @pl.when(pl.program_id(1) == 0)
    def _():
        st_ref[...] = jnp.zeros_like(st_ref)

    width = r_ref.shape[2]
    nc = r_ref.shape[1] // c
    hd = RW_HEAD_DIM

    lw_all = lw_ref[0]
    rows = lax.broadcasted_iota(jnp.int32, lw_all.shape, 0) & (c - 1)
    cum_all = lw_all
    step = 1
    while step < c:
        cum_all = cum_all + jnp.where(rows >= step, pltpu.roll(cum_all, step, 0), 0.0)
        step *= 2

    lane = lax.broadcasted_iota(jnp.int32, (c, LANES), 1)
    first = lane < hd
    r2 = lax.broadcasted_iota(jnp.int32, (2 * c, 2 * c), 0)
    c2 = lax.broadcasted_iota(jnp.int32, (2 * c, 2 * c), 1)
    same_head = (r2 >= c) == (c2 >= c)
    t_row = r2 & (c - 1)
    t_col = c2 & (c - 1)
    strict = same_head & (t_col < t_row)
    lower = same_head & (t_col <= t_row)

    def blk(x):
        return jnp.concatenate([jnp.where(first, x, 0.0), jnp.where(first, 0.0, x)],
                               axis=0).astype(BF16)

    def head_sum(x):
        s0 = jnp.sum(jnp.where(first, x, 0.0), axis=1, keepdims=True)
        s1 = jnp.sum(jnp.where(first, 0.0, x), axis=1, keepdims=True)
        return jnp.where(first, s0, s1)

    pairs = range(width // LANES)
    sls = [slice(p * LANES, (p + 1) * LANES) for p in pairs]
    units = [(j, p) for j in range(nc) for p in pairs]
    n = 2 * c
    la, lr, vb, bk_t, bk_h, p_last, vals, bonus_in = {}, {}, {}, {}, {}, {}, {}, {}
    for j in range(nc):
        rs = slice(j * c, (j + 1) * c)
        r, kf, v = r_ref[0, rs, :], k_ref[0, rs, :], v_ref[0, rs, :]
        kn, a = kn_ref[0, rs, :], a_ref[0, rs, :]
        lw, cum = lw_all[rs], cum_all[rs]
        cum_last = cum[c - 1:c, :]
        p_inv = jnp.exp(-cum)
        p_rem = jnp.exp(cum_last - cum)
        kb = kn * a
        a_t = -kn * jnp.exp(cum - lw)
        r_t = r * jnp.exp(cum)
        b_t = kb * p_inv
        k_t = kf * p_inv
        b_h = kb * p_rem
        k_h = kf * p_rem
        p_last[j] = jnp.exp(cum_last)
        vals[j] = v
        bonus_in[j] = r * kf * rk_ref[...]
        for p in pairs:
            sl = sls[p]
            la[j, p], lr[j, p], vb[j, p] = blk(a_t[:, sl]), blk(r_t[:, sl]), blk(v[:, sl])
            bk_t[j, p] = jnp.concatenate([blk(b_t[:, sl]), blk(k_t[:, sl])], axis=0)
            bk_h[j, p] = jnp.concatenate([blk(b_h[:, sl]), blk(k_h[:, sl])], axis=0)
    gram = {u: _dot_nt(jnp.concatenate([la[u], lr[u]], axis=0), bk_t[u]) for u in units}
    x = {u: jnp.where(strict, gram[u][:n, :n], 0.0) for u in units}
    npow = dict(x)
    for _ in range(int(math.log2(c)) - 1):
        npow_b = {u: npow[u].astype(BF16) for u in units}
        npow = {u: _dot(npow_b[u], npow_b[u]) for u in units}
        x = {u: x[u] + npow[u] + _dot(x[u].astype(BF16), npow[u].astype(BF16)) for u in units}
    x_b = {u: x[u].astype(BF16) for u in units}
    a_ak = {u: jnp.where(strict, gram[u][:n, n:], 0.0).astype(BF16) for u in units}
    lower2 = jnp.concatenate([lower, lower], axis=1)
    a_rbk = {u: jnp.where(lower2, gram[u][n:, :], 0.0).astype(BF16) for u in units}
    akv = {u: _dot(a_ak[u], vb[u]) for u in units}

    st = [st_ref[p] for p in pairs]
    for j in range(nc):
        st_b = [s.astype(BF16) for s in st]
        rhs = [_dot_nt(la[j, p], st_b[p]) + akv[j, p] for p in pairs]
        sa = [rhs[p] + _dot(x_b[j, p], rhs[p].astype(BF16)) for p in pairs]
        sav = [jnp.concatenate([sa[p].astype(BF16), vb[j, p]], axis=0) for p in pairs]
        yb = [_dot_nt(lr[j, p], st_b[p]) + _dot(a_rbk[j, p], sav[p]) for p in pairs]
        st = [st[p] * p_last[j][:, sls[p]] + _dot_tn(sav[p], bk_h[j, p]) for p in pairs]
        rs = slice(j * c, (j + 1) * c)
        for p in pairs:
            sl = sls[p]
            y = yb[p][:c] + yb[p][c:]
            mean = head_sum(y) * (1.0 / hd)
            yc = y - mean
            var = head_sum(yc * yc) * (1.0 / hd)
            yn = yc * lax.rsqrt(var + RW_GN_EPS) * lnw_ref[:, sl] + lnb_ref[:, sl]
            bonus = head_sum(bonus_in[j][:, sl]) * vals[j][:, sl]
            y_ref[0, rs, sl] = ((yn + bonus) * g_ref[0, rs, sl]).astype(BF16)
    for p in pairs:
        st_ref[p] = st[p]


def _rwkv(r, lw, kf, v, kn, a, g, r_k, ln_w, ln_b):
    b, s, w = r.shape
    c = min(RW_CHUNK, s)
    rows = min(RW_CHUNKS_PER_STEP * c, s)
    row = lambda i, j: (i, j, 0)
    const = lambda i, j: (0, 0)
    seq = pl.BlockSpec((1, rows, w), row)
    vec = pl.BlockSpec((1, w), const)
    return pl.pallas_call(
        functools.partial(_rwkv_kernel, c=c),
        grid=(b, s // rows),
        in_specs=[seq] * 7 + [vec] * 3,
        out_specs=seq,
        out_shape=jax.ShapeDtypeStruct((b, s, w), BF16),
        scratch_shapes=[pltpu.VMEM((w // LANES, LANES, LANES), F32)],
        compiler_params=_cparams(("arbitrary", "arbitrary")),
        name="rwkv7",
    )(r, lw, kf, v, kn, a, g, r_k, ln_w, ln_b)


def _merge_kernel(h_ref, ya_ref, yb_ref, yc_ref, gt_ref, wb_ref, wo_ref, o_ref):
    d = h_ref.shape[2]
    mixed = None
    for i, y_ref in enumerate((ya_ref, yb_ref, yc_ref)):
        up = _dot(y_ref[0], wb_ref[i])
        term = gt_ref[0, :, i * d:(i + 1) * d].astype(F32) * up
        mixed = term if mixed is None else mixed + term
    o_ref[0] = h_ref[0] + _dot(mixed.astype(BF16), wo_ref[...])


def _merge(h, ya, yb, yc, gates, wb, wo, ts):
    b, s, d = h.shape
    row = lambda i, j: (i, j, 0)
    bw = ya.shape[2]
    return pl.pallas_call(
        _merge_kernel,
        grid=(b, s // ts),
        in_specs=[pl.BlockSpec((1, ts, d), row)] + [pl.BlockSpec((1, ts, bw), row)] * 3
                 + [pl.BlockSpec((1, ts, N_BRANCH * d), row),
                    pl.BlockSpec((N_BRANCH, bw, d), lambda i, j: (0, 0, 0)),
                    pl.BlockSpec((d, d), lambda i, j: (0, 0))],
        out_specs=pl.BlockSpec((1, ts, d), row),
        out_shape=jax.ShapeDtypeStruct((b, s, d), F32),
        compiler_params=_cparams(("arbitrary", "arbitrary")),
        name="merge",
    )(h, ya, yb, yc, gates, wb, wo)


def _ffn_kernel(h_ref, g_ref, wup_ref, cw_ref, wd_ref, o_ref, act_ref, carry_ref):
    @pl.when(pl.program_id(1) == 0)
    def _():
        carry_ref[...] = jnp.zeros_like(carry_ref)

    ts = h_ref.shape[1]
    dff = wd_ref.shape[0]
    xn = _rms(h_ref[0], g_ref[...])

    def conv(sl):
        u = _dot(xn, wup_ref[:, sl])
        prev = carry_ref[:, sl]
        out = (_shift_rows(u, prev, 2) * cw_ref[0:1, sl] + _shift_rows(u, prev, 1) * cw_ref[1:2, sl]
               + u * cw_ref[2:3, sl])
        carry_ref[:, sl] = u[ts - SUBLANES:ts]
        return out

    for lo in range(0, dff, MXU_COLS):
        gate = conv(slice(lo, lo + MXU_COLS))
        up = conv(slice(dff + lo, dff + lo + MXU_COLS))
        act_ref[:, lo:lo + MXU_COLS] = (gate * jax.nn.sigmoid(gate) * up).astype(BF16)
    o_ref[0] = h_ref[0] + _dot(act_ref[...], wd_ref[...])


def _ffn(h, g, w_up, w_conv, w_down, ts):
    b, s, d = h.shape
    dff = w_down.shape[0]
    row = lambda i, j: (i, j, 0)
    const = lambda i, j: (0, 0)
    resident = dict(pipeline_mode=pl.Buffered(1))
    return pl.pallas_call(
        _ffn_kernel,
        grid=(b, s // ts),
        in_specs=[pl.BlockSpec((1, ts, d), row), pl.BlockSpec((1, d), const),
                  pl.BlockSpec((d, 2 * dff), const, **resident),
                  pl.BlockSpec((3, 2 * dff), const),
                  pl.BlockSpec((dff, d), const, **resident)],
        out_specs=pl.BlockSpec((1, ts, d), row),
        out_shape=jax.ShapeDtypeStruct((b, s, d), F32),
        scratch_shapes=[pltpu.VMEM((ts, dff), BF16), pltpu.VMEM((SUBLANES, 2 * dff), F32)],
        compiler_params=_cparams(("arbitrary", "arbitrary")),
        name="conv_glu_ffn",
    )(h, g, w_up, w_conv, w_down)


def _final_norm_kernel(h_ref, g_ref, o_ref):
    x = h_ref[0]
    ms = jnp.mean(x * x, axis=-1, keepdims=True)
    o_ref[0] = x * lax.rsqrt(ms + RMS_EPS) * g_ref[...]


def _final_norm(h, g, ts):
    b, s, d = h.shape
    row = lambda i, j: (i, j, 0)
    return pl.pallas_call(
        _final_norm_kernel,
        grid=(b, s // ts),
        in_specs=[pl.BlockSpec((1, ts, d), row), pl.BlockSpec((1, d), lambda i, j: (0, 0))],
        out_specs=pl.BlockSpec((1, ts, d), row),
        out_shape=jax.ShapeDtypeStruct((b, s, d), F32),
        compiler_params=_cparams(("arbitrary", "arbitrary")),
        name="final_norm",
    )(h, g)


def _rope_tables(positions, head_dim):
    rot = head_dim // ROPE_FRACTION
    half = rot // 2
    inv = ROPE_THETA ** (-jnp.arange(half, dtype=F32) * 2.0 / rot)
    ang = positions.astype(F32)[:, :, None] * inv
    cos, sin = jnp.cos(ang), jnp.sin(ang)
    zeros = jnp.zeros_like(cos)
    rest = head_dim - rot
    pad1 = jnp.ones(cos.shape[:2] + (rest,), F32)
    pad0 = jnp.zeros(cos.shape[:2] + (rest,), F32)
    reps = LANES // head_dim
    c = jnp.tile(jnp.concatenate([cos, cos, pad1], axis=-1), (1, 1, reps))
    s1 = jnp.tile(jnp.concatenate([-sin, zeros, pad0], axis=-1), (1, 1, reps))
    s2 = jnp.tile(jnp.concatenate([zeros, sin, pad0], axis=-1), (1, 1, reps))
    return c, s1, s2


def kernel(x, positions, norm_mix, w_in, b_gate, sc_conv, rw_mu, rw_w0, rw_w_up, rw_a0, rw_a_up, rw_g_up, rw_k_k, rw_k_a, rw_r_k, rw_ln_w, rw_ln_b, w_branch, w_out, norm_ffn, ffn_up, ffn_conv, ffn_down, norm_final):
    bsz, seq, d = x.shape
    depth = w_in.shape[0]
    ts = min(ROW_TILE, seq)
    k_sel = min(TOPK_MAX, seq // 4)

    tabs = _rope_tables(positions, A_HEAD_DIM) + _rope_tables(positions, IDX_DIM)
    o_q = 0
    o_k = o_q + A_WIDTH
    o_v = o_k + A_KV_WIDTH
    o_qi = o_v + A_KV_WIDTH
    o_ki = o_qi + IDX_WIDTH
    o_wi = o_ki + IDX_DIM
    o_sc = o_wi + IDX_HEADS
    o_rw = o_sc + 3 * SC_WIDTH
    o_gate = o_rw + RW_IN
    attn_scale = A_HEAD_DIM ** -0.5
    idx_scale = (IDX_HEADS ** -0.5) * (IDX_DIM ** -0.5)
    head_sum = jnp.kron(jnp.eye(MXU_COLS // RW_HEAD_DIM, dtype=F32),
                        jnp.ones((RW_HEAD_DIM, RW_HEAD_DIM), F32)).astype(BF16)

    h = x
    for l in range(depth):
        w = w_in[l]
        pad = jnp.zeros((d, LANES - IDX_DIM - IDX_HEADS), F32)
        w_attn = jnp.concatenate(
            [w[:, o_q:o_k] * attn_scale, w[:, o_k:o_ki], w[:, o_ki:o_wi],
             w[:, o_wi:o_sc] * idx_scale, pad], axis=1).astype(BF16)
        w_sc = w[:, o_sc:o_rw].astype(BF16)
        w_rw = w[:, o_rw:o_gate].astype(BF16)
        w_gate = w[:, o_gate:].astype(BF16)
        g_mix = norm_mix[l][None, :]

        q, k, v, qi, kw = _attn_proj(h, g_mix, w_attn, tabs, ts)
        y_b = _sc_proj(h, g_mix, w_sc, sc_conv[l], ts)
        zero_lora = jnp.zeros((RW_W_LORA, RW_WIDTH), F32)
        w_lora = jnp.concatenate(
            [jnp.concatenate([rw_w_up[l], zero_lora], axis=1),
             jnp.concatenate([zero_lora, rw_a_up[l]], axis=1)], axis=0).astype(BF16)
        r, lw, kf, vv, kn, a, gg = _rw_proj(
            h, g_mix, w_rw, rw_mu[l][None, :], w_lora, rw_g_up[l].astype(BF16),
            rw_w0[l][None, :], rw_a0[l][None, :], rw_k_k[l][None, :], rw_k_a[l][None, :],
            head_sum, ts)
        gates = _gate_proj(h, g_mix, w_gate, b_gate[l][None, :], ts)

        y_a = _dsa(q, k, v, qi, kw, k_sel)
        y_c = _rwkv(r, lw, kf, vv, kn, a, gg, rw_r_k[l].reshape(1, RW_WIDTH),
                    rw_ln_w[l][None, :], rw_ln_b[l][None, :])

        h = _merge(h, y_a, y_b, y_c, gates, w_branch[l].astype(BF16), w_out[l].astype(BF16), ts)
        h = _ffn(h, norm_ffn[l][None, :], ffn_up[l].astype(BF16), ffn_conv[l],
                 ffn_down[l].astype(BF16), ts)
    return _final_norm(h, norm_final[None, :], ts)
```

```python
import functools
import math

import jax
import jax.numpy as jnp
from jax import lax
from jax.experimental import pallas as pl
from jax.experimental.pallas import tpu as pltpu

F32 = jnp.float32
BF16 = jnp.bfloat16
HIGHEST = lax.Precision.HIGHEST

D_MODEL = 1024
A_HEADS = 8
A_KV_HEADS = 2
A_HEAD_DIM = 64
A_WIDTH = A_HEADS * A_HEAD_DIM
A_KV_WIDTH = A_KV_HEADS * A_HEAD_DIM
IDX_HEADS = 8
IDX_DIM = 32
IDX_WIDTH = IDX_HEADS * IDX_DIM
TOPK_MAX = 256
ROPE_THETA = 500000.0
ROPE_FRACTION = 4
SC_WIDTH = 512
RW_HEADS = 8
RW_HEAD_DIM = 64
RW_WIDTH = RW_HEADS * RW_HEAD_DIM
RW_W_LORA = 64
RW_A_LORA = 64
RW_G_LORA = 128
RW_IN = 3 * RW_WIDTH + RW_W_LORA + RW_A_LORA + RW_G_LORA
RW_GN_EPS = 64e-5
RW_DECAY_SCALE = math.exp(-0.5)
N_BRANCH = 3
D_FF = 2816
RMS_EPS = 1e-6

LANES = 128
MXU_COLS = 256
SUBLANES = 8
INT_MIN = -2147483648
VMEM_LIMIT = 48 * 1024 * 1024
IN_PROJ_VMEM_LIMIT = 56 * 1024 * 1024

ROW_TILE = 512
DSA_Q_TILE = 256
DSA_CLASSES = 4
RW_CHUNK = 64
RW_CHUNKS_PER_STEP = 4


def _cparams(sem):
    return pltpu.CompilerParams(dimension_semantics=sem, vmem_limit_bytes=VMEM_LIMIT)


def _rms(x, g):
    ms = jnp.mean(x * x, axis=-1, keepdims=True)
    return (x * lax.rsqrt(ms + RMS_EPS) * g).astype(BF16)


def _dot(a, b, precision=None):
    return jnp.dot(a, b, preferred_element_type=F32, precision=precision)


def _dot_nt(a, b, precision=None):
    return lax.dot_general(a, b, (((1,), (1,)), ((), ())), preferred_element_type=F32,
                           precision=precision)


def _dot_tn(a, b, precision=None):
    return lax.dot_general(a, b, (((0,), (0,)), ((), ())), preferred_element_type=F32,
                           precision=precision)


def _shift_rows(x, prev, k):
    xr = pltpu.roll(x, k, 0)
    pr = pltpu.roll(prev, k, 0)
    rows = lax.broadcasted_iota(jnp.int32, pr.shape, 0)
    head = jnp.where(rows < k, pr, xr[0:SUBLANES])
    return jnp.concatenate([head, xr[SUBLANES:]], axis=0)


def _rope_slab(y, c, s1, s2, half):
    return y * c + pltpu.roll(y, LANES - half, 1) * s1 + pltpu.roll(y, half, 1) * s2


def _attn_proj_body(xn, w_ref, cq_ref, s1q_ref, s2q_ref, ci_ref, s1i_ref, s2i_ref,
                    q_ref, k_ref, v_ref, qi_ref, kw_ref):
    cq, s1q, s2q = cq_ref[0], s1q_ref[0], s2q_ref[0]
    ci, s1i, s2i = ci_ref[0], s1i_ref[0], s2i_ref[0]
    hq = A_HEAD_DIM // ROPE_FRACTION // 2
    hi = IDX_DIM // ROPE_FRACTION // 2
    col = 0
    for p in range(A_WIDTH // MXU_COLS):
        y = _dot(xn, w_ref[:, col:col + MXU_COLS])
        for half in range(MXU_COLS // LANES):
            lo = p * MXU_COLS + half * LANES
            q_ref[0, :, lo:lo + LANES] = _rope_slab(
                y[:, half * LANES:(half + 1) * LANES], cq, s1q, s2q, hq).astype(BF16)
        col += MXU_COLS
    y = _dot(xn, w_ref[:, col:col + 2 * LANES])
    k_ref[0] = _rope_slab(y[:, :LANES], cq, s1q, s2q, hq).astype(BF16)
    val = y[:, LANES:]
    swapped = pltpu.roll(val, A_HEAD_DIM, 1)
    low = lax.broadcasted_iota(jnp.int32, val.shape, 1) < A_HEAD_DIM
    slabs = (jnp.where(low, val, 1.0), jnp.where(low, 1.0, swapped),
             jnp.where(low, swapped, 1.0), jnp.where(low, 1.0, val))
    for i, slab in enumerate(slabs):
        v_ref[0, :, i * LANES:(i + 1) * LANES] = slab.astype(BF16)
    col += 2 * LANES
    y = _dot(xn, w_ref[:, col:col + IDX_WIDTH])
    for half in range(IDX_WIDTH // LANES):
        qi_ref[0, :, half * LANES:(half + 1) * LANES] = _rope_slab(
            y[:, half * LANES:(half + 1) * LANES], ci, s1i, s2i, hi).astype(BF16)
    col += IDX_WIDTH
    y = _dot(xn, w_ref[:, col:col + LANES])
    lane = lax.broadcasted_iota(jnp.int32, y.shape, 1)
    is_key = lane < IDX_DIM
    ck = jnp.where(is_key, ci, 1.0)
    kw_ref[0] = _rope_slab(y, ck, jnp.where(is_key, s1i, 0.0), jnp.where(is_key, s2i, 0.0), hi)


def _sc_proj_body(xn, w_ref, cw_ref, y_ref, carry_ref):
    ts = xn.shape[0]
    wd = MXU_COLS
    for lo in range(0, SC_WIDTH, wd):
        u = _dot(xn, w_ref[:, lo:lo + wd])
        gate_b = _dot(xn, w_ref[:, SC_WIDTH + lo:SC_WIDTH + lo + wd])
        gate_c = _dot(xn, w_ref[:, 2 * SC_WIDTH + lo:2 * SC_WIDTH + lo + wd])
        cu = gate_c * u
        prev = carry_ref[:, lo:lo + wd]
        conv = (_shift_rows(cu, prev, 2) * cw_ref[0:1, lo:lo + wd]
                + _shift_rows(cu, prev, 1) * cw_ref[1:2, lo:lo + wd]
                + cu * cw_ref[2:3, lo:lo + wd])
        y_ref[0, :, lo:lo + wd] = (gate_b * conv).astype(BF16)
        carry_ref[:, lo:lo + wd] = cu[ts - SUBLANES:ts]


def _split_dot(x, m):
    hi = x.astype(BF16)
    lo = (x - hi.astype(F32)).astype(BF16)
    return _dot(hi, m) + _dot(lo, m)


def _rw_proj_body(xn, w_ref, mu_ref, wlora_ref, gup_ref, w0_ref, a0_ref, kk_ref, ka_ref, hsum_ref,
                  r_ref, lw_ref, k_ref, v_ref, kn_ref, a_ref, gg_ref, carry_ref):
    ts = xn.shape[0]

    def mixed(lo, width):
        z = _dot(xn, w_ref[:, lo:lo + width])
        zp = _shift_rows(z, carry_ref[:, lo:lo + width], 1)
        carry_ref[:, lo:lo + width] = z[ts - SUBLANES:ts]
        return z + (zp - z) * mu_ref[:, lo:lo + width]

    lora_lo = 3 * RW_WIDTH
    zlg = mixed(lora_lo, RW_IN - lora_lo)
    zl = zlg[:, :LANES]
    lane = lax.broadcasted_iota(jnp.int32, zl.shape, 1)
    zl = jnp.where(lane < RW_W_LORA, jnp.tanh(zl), zl).astype(BF16)
    zg = jax.nn.sigmoid(zlg[:, LANES:]).astype(BF16)
    wd = MXU_COLS
    for lo in range(0, RW_WIDTH, wd):
        sl = slice(lo, lo + wd)
        up = _dot(zl, wlora_ref[:, lo:lo + wd])
        ua = _dot(zl, wlora_ref[:, RW_WIDTH + lo:RW_WIDTH + lo + wd])
        lw_ref[0, :, sl] = -RW_DECAY_SCALE * jax.nn.sigmoid(w0_ref[:, sl] + up)
        a = jax.nn.sigmoid(a0_ref[:, sl] + ua)
        a_ref[0, :, sl] = a.astype(BF16)
        gg_ref[0, :, sl] = _dot(zg, gup_ref[:, sl]).astype(BF16)
        r_ref[0, :, sl] = mixed(lo, wd).astype(BF16)
        k = mixed(RW_WIDTH + lo, wd)
        v_ref[0, :, sl] = mixed(2 * RW_WIDTH + lo, wd).astype(BF16)
        kk = k * kk_ref[:, sl]
        ss = _split_dot(kk * kk, hsum_ref[...])
        kn_ref[0, :, sl] = (kk * lax.rsqrt(jnp.maximum(ss, 1e-24))).astype(BF16)
        k_ref[0, :, sl] = (k * (1.0 + (a - 1.0) * ka_ref[:, sl])).astype(BF16)


def _gate_proj_body(xn, w_ref, b_ref, o_ref):
    n = w_ref.shape[1]
    step = 4 * LANES
    for lo in range(0, n, step):
        y = _dot(xn, w_ref[:, lo:lo + step]) + b_ref[:, lo:lo + step]
        o_ref[0, :, lo:lo + step] = jax.nn.sigmoid(y).astype(BF16)


N_ATTN_IN, N_SC_IN, N_RW_IN, N_GATE_IN = 7, 2, 9, 2
N_ATTN_OUT, N_SC_OUT, N_RW_OUT = 5, 1, 7


def _in_proj_kernel(*refs):
    h_ref, g_ref = refs[:2]
    pos = 2
    attn_in = refs[pos:pos + N_ATTN_IN]; pos += N_ATTN_IN
    sc_in = refs[pos:pos + N_SC_IN]; pos += N_SC_IN
    rw_in = refs[pos:pos + N_RW_IN]; pos += N_RW_IN
    gate_in = refs[pos:pos + N_GATE_IN]; pos += N_GATE_IN
    attn_out = refs[pos:pos + N_ATTN_OUT]; pos += N_ATTN_OUT
    sc_out = refs[pos:pos + N_SC_OUT]; pos += N_SC_OUT
    rw_out = refs[pos:pos + N_RW_OUT]; pos += N_RW_OUT
    gate_out = refs[pos]; pos += 1
    sc_carry, rw_carry = refs[pos:pos + 2]

    @pl.when(pl.program_id(1) == 0)
    def _():
        sc_carry[...] = jnp.zeros_like(sc_carry)
        rw_carry[...] = jnp.zeros_like(rw_carry)

    xn = _rms(h_ref[0], g_ref[...])
    _attn_proj_body(xn, *attn_in, *attn_out)
    _sc_proj_body(xn, *sc_in, *sc_out, sc_carry)
    _rw_proj_body(xn, *rw_in, *rw_out, rw_carry)
    _gate_proj_body(xn, *gate_in, gate_out)


def _in_proj(h, g, attn_in, sc_in, rw_in, gate_in, ts):
    b, s, d = h.shape
    row = lambda i, j: (i, j, 0)
    const = lambda i, j: (0, 0)

    def seq_spec(width):
        return pl.BlockSpec((1, ts, width), row)

    def whole(x):
        if x.size * x.dtype.itemsize >= (1 << 20):
            return pl.BlockSpec(x.shape, const, pipeline_mode=pl.Buffered(1))
        return pl.BlockSpec(x.shape, const)

    w_attn, tabs = attn_in[0], attn_in[1:]
    in_specs = ([seq_spec(d), whole(g), whole(w_attn)] + [seq_spec(LANES)] * len(tabs)
                + [whole(x) for x in sc_in] + [whole(x) for x in rw_in]
                + [whole(x) for x in gate_in])
    out_widths = ([(A_WIDTH, BF16), (LANES, BF16), (4 * LANES, BF16), (IDX_WIDTH, BF16),
                   (LANES, F32), (SC_WIDTH, BF16)]
                  + [(RW_WIDTH, F32 if i == 1 else BF16) for i in range(N_RW_OUT)]
                  + [(gate_in[0].shape[1], BF16)])
    outs = pl.pallas_call(
        _in_proj_kernel,
        grid=(b, s // ts),
        in_specs=in_specs,
        out_specs=[seq_spec(w) for w, _ in out_widths],
        out_shape=[jax.ShapeDtypeStruct((b, s, w), dt) for w, dt in out_widths],
        scratch_shapes=[pltpu.VMEM((SUBLANES, SC_WIDTH), F32), pltpu.VMEM((SUBLANES, RW_IN), F32)],
        compiler_params=pltpu.CompilerParams(dimension_semantics=("arbitrary", "arbitrary"),
                                             vmem_limit_bytes=IN_PROJ_VMEM_LIMIT),
        name="in_proj",
    )(h, g, *attn_in, *sc_in, *rw_in, *gate_in)
    return outs[:5], outs[5], outs[6:13], outs[13]


def _dsa_kernel(q_ref, k_ref, v_ref, qi_ref, kwk_ref, kwq_ref, o_ref,
                key_ref, hi_ref, lo_ref, bias_ref, *, first_block, k_sel):
    tq = q_ref.shape[1]
    sk = k_ref.shape[1]
    t0 = (first_block + pl.program_id(1)) * tq

    ki = kwk_ref[0][:, 0:IDX_DIM].astype(BF16)
    wq = kwq_ref[0]
    qi = qi_ref[0]
    isc = jnp.zeros((tq, sk), F32)
    for h in range(IDX_HEADS):
        rel = jnp.maximum(_dot_nt(qi[:, h * IDX_DIM:(h + 1) * IDX_DIM], ki), 0.0)
        isc = isc + wq[:, IDX_DIM + h:IDX_DIM + h + 1] * rel

    cols = lax.broadcasted_iota(jnp.int32, (tq, sk), 1)
    rows = t0 + lax.broadcasted_iota(jnp.int32, (tq, sk), 0)
    causal = cols <= rows
    bits = pltpu.bitcast(isc, jnp.int32)
    key = bits ^ ((bits >> 31) & jnp.int32(0x7FFFFFFF))
    key = jnp.where(causal, key, INT_MIN)
    key_ref[...] = key

    half_min = -(1 << 15)

    def search_half(src_ref, cnt0):
        def body(it, carry):
            tau_u, cnt_tau = carry
            cand_u = tau_u | jnp.left_shift(jnp.int32(1), 15 - it)
            cand = jnp.broadcast_to(cand_u + half_min, (tq, LANES)).astype(jnp.int16)
            accs = [jnp.zeros((tq, LANES), jnp.int16) for _ in range(4)]
            for j in range(sk // LANES):
                hit = src_ref[:, j * LANES:(j + 1) * LANES] >= cand
                accs[j % 4] = accs[j % 4] + jnp.where(hit, jnp.int16(1), jnp.int16(0))
            acc = (accs[0] + accs[1]) + (accs[2] + accs[3])
            cnt = jnp.sum(acc.astype(F32), axis=1, keepdims=True)
            ok = cnt >= k_sel
            return jnp.where(ok, cand_u, tau_u), jnp.where(ok, cnt, cnt_tau)
        return lax.fori_loop(0, 16, body, (jnp.zeros((tq, 1), jnp.int32), cnt0), unroll=True)

    hi_ref[...] = (key >> 16).astype(jnp.int16)
    lo_ref[...] = ((key & jnp.int32(0xFFFF)) + half_min).astype(jnp.int16)
    tau_hi_u, cnt_hi = search_half(hi_ref, jnp.full((tq, 1), float(sk), F32))
    tau_hi = tau_hi_u + half_min
    t_hi = jnp.broadcast_to(tau_hi, (tq, LANES)).astype(jnp.int16)
    for j in range(sk // LANES):
        sl = slice(j * LANES, (j + 1) * LANES)
        hi_j = hi_ref[:, sl]
        inside = jnp.where(hi_j == t_hi, lo_ref[:, sl], jnp.int16(half_min))
        lo_ref[:, sl] = jnp.where(hi_j > t_hi, jnp.int16(-half_min - 1), inside)
    tau_lo_u, cnt_ge = search_half(lo_ref, cnt_hi)
    tau = (tau_hi << 16) | tau_lo_u

    drop = jnp.where(tau == INT_MIN, float(2 * sk), cnt_ge - k_sel)
    r2 = lax.broadcasted_iota(jnp.int32, (LANES, 2 * LANES), 0)
    c2 = lax.broadcasted_iota(jnp.int32, (LANES, 2 * LANES), 1)
    tri = jnp.where((r2 > c2) | (c2 >= LANES), 1.0, 0.0).astype(BF16)
    after = jnp.zeros((tq, LANES), F32)
    for j in reversed(range(sk // LANES)):
        sl = slice(j * LANES, (j + 1) * LANES)
        key_j = key_ref[:, sl]
        eq = key_j == tau
        pr = _dot(jnp.where(eq, 1.0, 0.0).astype(BF16), tri)
        tied = jnp.where(eq, jnp.where(after + pr[:, :LANES] >= drop, 0.0, -jnp.inf), -jnp.inf)
        bias_ref[:, sl] = jnp.where(key_j > tau, 0.0, tied)
        after = after + pr[:, LANES:]

    bias = bias_ref[...]
    kk = k_ref[0]
    group = A_HEADS // A_KV_HEADS
    low = lax.broadcasted_iota(jnp.int32, (tq, LANES), 1) < A_HEAD_DIM
    for h2 in range(A_HEADS // 2):
        res = []
        for h in (2 * h2, 2 * h2 + 1):
            n = h // group
            kn = kk[:, n * A_HEAD_DIM:(n + 1) * A_HEAD_DIM]
            s = _dot_nt(q_ref[0, :, h * A_HEAD_DIM:(h + 1) * A_HEAD_DIM], kn) + bias
            m = jnp.max(s, axis=1, keepdims=True)
            p = jnp.exp(s - m).astype(BF16)
            vx = v_ref[0, :, (2 * n + h % 2) * LANES:(2 * n + h % 2 + 1) * LANES]
            oe = _dot(p, vx)
            res.append(oe / pltpu.roll(oe, A_HEAD_DIM, 1))
        o_ref[0, :, h2 * LANES:(h2 + 1) * LANES] = jnp.where(low, res[0], res[1]).astype(BF16)


def _dsa(q, k, v, qi, kw, k_sel):
    b, s, _ = q.shape
    tq = min(DSA_Q_TILE, s)
    nblk = s // tq
    per = max(1, nblk // DSA_CLASSES)
    outs = []
    for first in range(0, nblk, per):
        sk = (first + per) * tq
        qmap = lambda i, j, first=first: (i, first + j, 0)
        kmap = lambda i, j: (i, 0, 0)
        outs.append(pl.pallas_call(
            functools.partial(_dsa_kernel, first_block=first, k_sel=k_sel),
            grid=(b, per),
            in_specs=[pl.BlockSpec((1, tq, A_WIDTH), qmap), pl.BlockSpec((1, sk, LANES), kmap),
                      pl.BlockSpec((1, sk, 4 * LANES), kmap), pl.BlockSpec((1, tq, IDX_WIDTH), qmap),
                      pl.BlockSpec((1, sk, LANES), kmap), pl.BlockSpec((1, tq, LANES), qmap)],
            out_specs=pl.BlockSpec((1, tq, A_WIDTH), lambda i, j: (i, j, 0)),
            out_shape=jax.ShapeDtypeStruct((b, per * tq, A_WIDTH), BF16),
            scratch_shapes=[pltpu.VMEM((tq, sk), jnp.int32), pltpu.VMEM((tq, sk), jnp.int16),
                            pltpu.VMEM((tq, sk), jnp.int16), pltpu.VMEM((tq, sk), F32)],
            compiler_params=_cparams(("arbitrary", "arbitrary")),
            name=f"dsa_{sk}",
        )(q, k, v, qi, kw, kw))
    return jnp.concatenate(outs, axis=1)


def _rwkv_kernel(r_ref, lw_ref, k_ref, v_ref, kn_ref, a_ref, g_ref, rk_ref, lnw_ref, lnb_ref,
                 y_ref, st_ref, *, c):
    @pl.when(pl.program_id(1) == 0)
    def _():
        st_ref[...] = jnp.zeros_like(st_ref)

    width = r_ref.shape[2]
    nc = r_ref.shape[1] // c
    hd = RW_HEAD_DIM

    lw_all = lw_ref[0]
    rows = lax.broadcasted_iota(jnp.int32, lw_all.shape, 0) & (c - 1)
    cum_all = lw_all
    step = 1
    while step < c:
        cum_all = cum_all + jnp.where(rows >= step, pltpu.roll(cum_all, step, 0), 0.0)
        step *= 2

    lane = lax.broadcasted_iota(jnp.int32, (c, LANES), 1)
    first = lane < hd
    r2 = lax.broadcasted_iota(jnp.int32, (2 * c, 2 * c), 0)
    c2 = lax.broadcasted_iota(jnp.int32, (2 * c, 2 * c), 1)
    same_head = (r2 >= c) == (c2 >= c)
    t_row = r2 & (c - 1)
    t_col = c2 & (c - 1)
    strict = same_head & (t_col < t_row)
    lower = same_head & (t_col <= t_row)

    def blk(x):
        return jnp.concatenate([jnp.where(first, x, 0.0), jnp.where(first, 0.0, x)],
                               axis=0).astype(BF16)

    def head_sum(x):
        s0 = jnp.sum(jnp.where(first, x, 0.0), axis=1, keepdims=True)
        s1 = jnp.sum(jnp.where(first, 0.0, x), axis=1, keepdims=True)
        return jnp.where(first, s0, s1)

    pairs = range(width // LANES)
    sls = [slice(p * LANES, (p + 1) * LANES) for p in pairs]
    units = [(j, p) for j in range(nc) for p in pairs]
    n = 2 * c
    la, lr, vb, bk_t, bk_h, p_last, vals, bonus_in = {}, {}, {}, {}, {}, {}, {}, {}
    for j in range(nc):
        rs = slice(j * c, (j + 1) * c)
        r, kf, v = (x[0, rs, :].astype(F32) for x in (r_ref, k_ref, v_ref))
        kn, a = kn_ref[0, rs, :].astype(F32), a_ref[0, rs, :].astype(F32)
        lw, cum = lw_all[rs], cum_all[rs]
        cum_last = cum[c - 1:c, :]
        p_inv = jnp.exp(-cum)
        p_rem = jnp.exp(cum_last - cum)
        kb = kn * a
        a_t = -kn * jnp.exp(cum - lw)
        r_t = r * jnp.exp(cum)
        b_t = kb * p_inv
        k_t = kf * p_inv
        b_h = kb * p_rem
        k_h = kf * p_rem
        p_last[j] = jnp.exp(cum_last)
        vals[j] = v
        bonus_in[j] = r * kf * rk_ref[...]
        for p in pairs:
            sl = sls[p]
            la[j, p], lr[j, p], vb[j, p] = blk(a_t[:, sl]), blk(r_t[:, sl]), blk(v[:, sl])
            bk_t[j, p] = jnp.concatenate([blk(b_t[:, sl]), blk(k_t[:, sl])], axis=0)
            bk_h[j, p] = jnp.concatenate([blk(b_h[:, sl]), blk(k_h[:, sl])], axis=0)
    gram = {u: _dot_nt(jnp.concatenate([la[u], lr[u]], axis=0), bk_t[u]) for u in units}
    x = {u: jnp.where(strict, gram[u][:n, :n], 0.0) for u in units}
    npow = dict(x)
    for _ in range(int(math.log2(c)) - 1):
        npow_b = {u: npow[u].astype(BF16) for u in units}
        npow = {u: _dot(npow_b[u], npow_b[u]) for u in units}
        x = {u: x[u] + npow[u] + _dot(x[u].astype(BF16), npow[u].astype(BF16)) for u in units}
    x_b = {u: x[u].astype(BF16) for u in units}
    a_ak = {u: jnp.where(strict, gram[u][:n, n:], 0.0).astype(BF16) for u in units}
    lower2 = jnp.concatenate([lower, lower], axis=1)
    a_rbk = {u: jnp.where(lower2, gram[u][n:, :], 0.0).astype(BF16) for u in units}
    akv = {u: _dot(a_ak[u], vb[u]) for u in units}

    st = [st_ref[p] for p in pairs]
    for j in range(nc):
        st_b = [s.astype(BF16) for s in st]
        rhs = [_dot_nt(la[j, p], st_b[p]) + akv[j, p] for p in pairs]
        sa = [rhs[p] + _dot(x_b[j, p], rhs[p].astype(BF16)) for p in pairs]
        sav = [jnp.concatenate([sa[p].astype(BF16), vb[j, p]], axis=0) for p in pairs]
        yb = [_dot_nt(lr[j, p], st_b[p]) + _dot(a_rbk[j, p], sav[p]) for p in pairs]
        st = [st[p] * p_last[j][:, sls[p]] + _dot_tn(sav[p], bk_h[j, p]) for p in pairs]
        rs = slice(j * c, (j + 1) * c)
        for p in pairs:
            sl = sls[p]
            y = yb[p][:c] + yb[p][c:]
            mean = head_sum(y) * (1.0 / hd)
            yc = y - mean
            var = head_sum(yc * yc) * (1.0 / hd)
            yn = yc * lax.rsqrt(var + RW_GN_EPS) * lnw_ref[:, sl] + lnb_ref[:, sl]
            bonus = head_sum(bonus_in[j][:, sl]) * vals[j][:, sl]
            y_ref[0, rs, sl] = ((yn + bonus) * g_ref[0, rs, sl]).astype(BF16)
    for p in pairs:
        st_ref[p] = st[p]


def _rwkv(r, lw, kf, v, kn, a, g, r_k, ln_w, ln_b):
    b, s, w = r.shape
    c = min(RW_CHUNK, s)
    rows = min(RW_CHUNKS_PER_STEP * c, s)
    row = lambda i, j: (i, j, 0)
    const = lambda i, j: (0, 0)
    seq = pl.BlockSpec((1, rows, w), row)
    vec = pl.BlockSpec((1, w), const)
    return pl.pallas_call(
        functools.partial(_rwkv_kernel, c=c),
        grid=(b, s // rows),
        in_specs=[seq] * 7 + [vec] * 3,
        out_specs=seq,
        out_shape=jax.ShapeDtypeStruct((b, s, w), BF16),
        scratch_shapes=[pltpu.VMEM((w // LANES, LANES, LANES), F32)],
        compiler_params=_cparams(("arbitrary", "arbitrary")),
        name="rwkv7",
    )(r, lw, kf, v, kn, a, g, r_k, ln_w, ln_b)


def _merge_kernel(h_ref, ya_ref, yb_ref, yc_ref, gt_ref, wb_ref, wo_ref, o_ref):
    d = h_ref.shape[2]
    mixed = None
    for i, y_ref in enumerate((ya_ref, yb_ref, yc_ref)):
        up = _dot(y_ref[0], wb_ref[i])
        term = gt_ref[0, :, i * d:(i + 1) * d].astype(F32) * up
        mixed = term if mixed is None else mixed + term
    o_ref[0] = h_ref[0] + _dot(mixed.astype(BF16), wo_ref[...])


def _merge(h, ya, yb, yc, gates, wb, wo, ts):
    b, s, d = h.shape
    row = lambda i, j: (i, j, 0)
    bw = ya.shape[2]
    return pl.pallas_call(
        _merge_kernel,
        grid=(b, s // ts),
        in_specs=[pl.BlockSpec((1, ts, d), row)] + [pl.BlockSpec((1, ts, bw), row)] * 3
                 + [pl.BlockSpec((1, ts, N_BRANCH * d), row),
                    pl.BlockSpec((N_BRANCH, bw, d), lambda i, j: (0, 0, 0)),
                    pl.BlockSpec((d, d), lambda i, j: (0, 0))],
        out_specs=pl.BlockSpec((1, ts, d), row),
        out_shape=jax.ShapeDtypeStruct((b, s, d), F32),
        compiler_params=_cparams(("arbitrary", "arbitrary")),
        name="merge",
    )(h, ya, yb, yc, gates, wb, wo)


def _ffn_kernel(h_ref, g_ref, wup_ref, cw_ref, wd_ref, gf_ref, o_ref, act_ref, carry_ref, *,
                final_norm):
    @pl.when(pl.program_id(1) == 0)
    def _():
        carry_ref[...] = jnp.zeros_like(carry_ref)

    ts = h_ref.shape[1]
    dff = wd_ref.shape[0]
    xn = _rms(h_ref[0], g_ref[...])

    def conv(sl):
        u = _dot(xn, wup_ref[:, sl])
        prev = carry_ref[:, sl]
        out = (_shift_rows(u, prev, 2) * cw_ref[0:1, sl] + _shift_rows(u, prev, 1) * cw_ref[1:2, sl]
               + u * cw_ref[2:3, sl])
        carry_ref[:, sl] = u[ts - SUBLANES:ts]
        return out

    for lo in range(0, dff, MXU_COLS):
        gate = conv(slice(lo, lo + MXU_COLS))
        up = conv(slice(dff + lo, dff + lo + MXU_COLS))
        act_ref[:, lo:lo + MXU_COLS] = (gate * jax.nn.sigmoid(gate) * up).astype(BF16)
    out = h_ref[0] + _dot(act_ref[...], wd_ref[...])
    if final_norm:
        ms = jnp.mean(out * out, axis=-1, keepdims=True)
        out = out * lax.rsqrt(ms + RMS_EPS) * gf_ref[...]
    o_ref[0] = out


def _ffn(h, g, w_up, w_conv, w_down, g_final, final_norm, ts):
    b, s, d = h.shape
    dff = w_down.shape[0]
    row = lambda i, j: (i, j, 0)
    const = lambda i, j: (0, 0)
    resident = dict(pipeline_mode=pl.Buffered(1))
    return pl.pallas_call(
        functools.partial(_ffn_kernel, final_norm=final_norm),
        grid=(b, s // ts),
        in_specs=[pl.BlockSpec((1, ts, d), row), pl.BlockSpec((1, d), const),
                  pl.BlockSpec((d, 2 * dff), const, **resident),
                  pl.BlockSpec((3, 2 * dff), const),
                  pl.BlockSpec((dff, d), const, **resident),
                  pl.BlockSpec((1, d), const)],
        out_specs=pl.BlockSpec((1, ts, d), row),
        out_shape=jax.ShapeDtypeStruct((b, s, d), F32),
        scratch_shapes=[pltpu.VMEM((ts, dff), BF16), pltpu.VMEM((SUBLANES, 2 * dff), F32)],
        compiler_params=_cparams(("arbitrary", "arbitrary")),
        name="conv_glu_ffn",
    )(h, g, w_up, w_conv, w_down, g_final)


def _rope_tables(positions, head_dim):
    rot = head_dim // ROPE_FRACTION
    half = rot // 2
    inv = ROPE_THETA ** (-jnp.arange(half, dtype=F32) * 2.0 / rot)
    ang = positions.astype(F32)[:, :, None] * inv
    cos, sin = jnp.cos(ang), jnp.sin(ang)
    zeros = jnp.zeros_like(cos)
    rest = head_dim - rot
    pad1 = jnp.ones(cos.shape[:2] + (rest,), F32)
    pad0 = jnp.zeros(cos.shape[:2] + (rest,), F32)
    reps = LANES // head_dim
    c = jnp.tile(jnp.concatenate([cos, cos, pad1], axis=-1), (1, 1, reps))
    s1 = jnp.tile(jnp.concatenate([-sin, zeros, pad0], axis=-1), (1, 1, reps))
    s2 = jnp.tile(jnp.concatenate([zeros, sin, pad0], axis=-1), (1, 1, reps))
    return c, s1, s2


def kernel(x, positions, norm_mix, w_in, b_gate, sc_conv, rw_mu, rw_w0, rw_w_up, rw_a0, rw_a_up, rw_g_up, rw_k_k, rw_k_a, rw_r_k, rw_ln_w, rw_ln_b, w_branch, w_out, norm_ffn, ffn_up, ffn_conv, ffn_down, norm_final):
    bsz, seq, d = x.shape
    depth = w_in.shape[0]
    ts = min(ROW_TILE, seq)
    k_sel = min(TOPK_MAX, seq // 4)

    tabs = _rope_tables(positions, A_HEAD_DIM) + _rope_tables(positions, IDX_DIM)
    o_q = 0
    o_k = o_q + A_WIDTH
    o_v = o_k + A_KV_WIDTH
    o_qi = o_v + A_KV_WIDTH
    o_ki = o_qi + IDX_WIDTH
    o_wi = o_ki + IDX_DIM
    o_sc = o_wi + IDX_HEADS
    o_rw = o_sc + 3 * SC_WIDTH
    o_gate = o_rw + RW_IN
    attn_scale = A_HEAD_DIM ** -0.5
    idx_scale = (IDX_HEADS ** -0.5) * (IDX_DIM ** -0.5)
    head_sum = jnp.kron(jnp.eye(MXU_COLS // RW_HEAD_DIM, dtype=F32),
                        jnp.ones((RW_HEAD_DIM, RW_HEAD_DIM), F32)).astype(BF16)

    h = x
    for l in range(depth):
        w = w_in[l]
        pad = jnp.zeros((d, LANES - IDX_DIM - IDX_HEADS), F32)
        w_attn = jnp.concatenate(
            [w[:, o_q:o_k] * attn_scale, w[:, o_k:o_ki], w[:, o_ki:o_wi],
             w[:, o_wi:o_sc] * idx_scale, pad], axis=1).astype(BF16)
        w_sc = w[:, o_sc:o_rw].astype(BF16)
        w_rw = w[:, o_rw:o_gate].astype(BF16)
        w_gate = w[:, o_gate:].astype(BF16)
        g_mix = norm_mix[l][None, :]

        zero_lora = jnp.zeros((RW_W_LORA, RW_WIDTH), F32)
        w_lora = jnp.concatenate(
            [jnp.concatenate([rw_w_up[l], zero_lora], axis=1),
             jnp.concatenate([zero_lora, rw_a_up[l]], axis=1)], axis=0).astype(BF16)
        (q, k, v, qi, kw), y_b, (r, lw, kf, vv, kn, a, gg), gates = _in_proj(
            h, g_mix,
            (w_attn,) + tabs,
            (w_sc, sc_conv[l]),
            (w_rw, rw_mu[l][None, :], w_lora, rw_g_up[l].astype(BF16), rw_w0[l][None, :],
             rw_a0[l][None, :], rw_k_k[l][None, :], rw_k_a[l][None, :], head_sum),
            (w_gate, b_gate[l][None, :]), ts)

        y_a = _dsa(q, k, v, qi, kw, k_sel)
        y_c = _rwkv(r, lw, kf, vv, kn, a, gg, rw_r_k[l].reshape(1, RW_WIDTH),
                    rw_ln_w[l][None, :], rw_ln_b[l][None, :])

        h = _merge(h, y_a, y_b, y_c, gates, w_branch[l].astype(BF16), w_out[l].astype(BF16), ts)
        h = _ffn(h, norm_ffn[l][None, :], ffn_up[l].astype(BF16), ffn_conv[l],
                 ffn_down[l].astype(BF16), norm_final[None, :], l == depth - 1, ts)
    return h
```

```python
import functools
import math

import jax
import jax.numpy as jnp
from jax import lax
from jax.experimental import pallas as pl
from jax.experimental.pallas import tpu as pltpu

F32 = jnp.float32
BF16 = jnp.bfloat16
HIGHEST = lax.Precision.HIGHEST

D_MODEL = 1024
A_HEADS = 8
A_KV_HEADS = 2
A_HEAD_DIM = 64
A_WIDTH = A_HEADS * A_HEAD_DIM
A_KV_WIDTH = A_KV_HEADS * A_HEAD_DIM
IDX_HEADS = 8
IDX_DIM = 32
IDX_WIDTH = IDX_HEADS * IDX_DIM
TOPK_MAX = 256
ROPE_THETA = 500000.0
ROPE_FRACTION = 4
SC_WIDTH = 512
RW_HEADS = 8
RW_HEAD_DIM = 64
RW_WIDTH = RW_HEADS * RW_HEAD_DIM
RW_W_LORA = 64
RW_A_LORA = 64
RW_G_LORA = 128
RW_IN = 3 * RW_WIDTH + RW_W_LORA + RW_A_LORA + RW_G_LORA
RW_GN_EPS = 64e-5
RW_DECAY_SCALE = math.exp(-0.5)
N_BRANCH = 3
D_FF = 2816
RMS_EPS = 1e-6

LANES = 128
MXU_COLS = 256
SUBLANES = 8
INT_MIN = -2147483648
VMEM_LIMIT = 48 * 1024 * 1024
WIDE_VMEM_LIMIT = 56 * 1024 * 1024

ROW_TILE = 512
DSA_Q_TILE = 256
DSA_CLASSES = 8
DSA_BATCH_PER_STEP = 1
RW_CHUNK = 64
RW_CHUNKS_PER_STEP = 4


def _cparams(sem):
    return pltpu.CompilerParams(dimension_semantics=sem, vmem_limit_bytes=VMEM_LIMIT)


def _rms(x, g):
    ms = jnp.mean(x * x, axis=-1, keepdims=True)
    return (x * lax.rsqrt(ms + RMS_EPS) * g).astype(BF16)


def _dot(a, b, precision=None):
    return jnp.dot(a, b, preferred_element_type=F32, precision=precision)


def _dot_nt(a, b, precision=None):
    return lax.dot_general(a, b, (((1,), (1,)), ((), ())), preferred_element_type=F32,
                           precision=precision)


def _dot_tn(a, b, precision=None):
    return lax.dot_general(a, b, (((0,), (0,)), ((), ())), preferred_element_type=F32,
                           precision=precision)


def _shift_rows(x, prev, k):
    xr = pltpu.roll(x, k, 0)
    pr = pltpu.roll(prev, k, 0)
    rows = lax.broadcasted_iota(jnp.int32, pr.shape, 0)
    head = jnp.where(rows < k, pr, xr[0:SUBLANES])
    return jnp.concatenate([head, xr[SUBLANES:]], axis=0)


def _rope_slab(y, c, s1, s2, half):
    return y * c + pltpu.roll(y, LANES - half, 1) * s1 + pltpu.roll(y, half, 1) * s2


def _attn_proj_body(xn, w_ref, cq_ref, s1q_ref, s2q_ref, ci_ref, s1i_ref, s2i_ref,
                    q_ref, k_ref, v_ref, qi_ref, kw_ref):
    cq, s1q, s2q = cq_ref[0], s1q_ref[0], s2q_ref[0]
    ci, s1i, s2i = ci_ref[0], s1i_ref[0], s2i_ref[0]
    hq = A_HEAD_DIM // ROPE_FRACTION // 2
    hi = IDX_DIM // ROPE_FRACTION // 2
    col = 0
    for p in range(A_WIDTH // MXU_COLS):
        y = _dot(xn, w_ref[:, col:col + MXU_COLS])
        for half in range(MXU_COLS // LANES):
            lo = p * MXU_COLS + half * LANES
            q_ref[0, :, lo:lo + LANES] = _rope_slab(
                y[:, half * LANES:(half + 1) * LANES], cq, s1q, s2q, hq).astype(BF16)
        col += MXU_COLS
    y = _dot(xn, w_ref[:, col:col + 2 * LANES])
    k_ref[0] = _rope_slab(y[:, :LANES], cq, s1q, s2q, hq).astype(BF16)
    val = y[:, LANES:]
    swapped = pltpu.roll(val, A_HEAD_DIM, 1)
    low = lax.broadcasted_iota(jnp.int32, val.shape, 1) < A_HEAD_DIM
    slabs = (jnp.where(low, val, 1.0), jnp.where(low, 1.0, swapped),
             jnp.where(low, swapped, 1.0), jnp.where(low, 1.0, val))
    for i, slab in enumerate(slabs):
        v_ref[0, :, i * LANES:(i + 1) * LANES] = slab.astype(BF16)
    col += 2 * LANES
    y = _dot(xn, w_ref[:, col:col + IDX_WIDTH])
    for half in range(IDX_WIDTH // LANES):
        qi_ref[0, :, half * LANES:(half + 1) * LANES] = _rope_slab(
            y[:, half * LANES:(half + 1) * LANES], ci, s1i, s2i, hi).astype(BF16)
    col += IDX_WIDTH
    y = _dot(xn, w_ref[:, col:col + LANES])
    lane = lax.broadcasted_iota(jnp.int32, y.shape, 1)
    is_key = lane < IDX_DIM
    ck = jnp.where(is_key, ci, 1.0)
    kw_ref[0] = _rope_slab(y, ck, jnp.where(is_key, s1i, 0.0), jnp.where(is_key, s2i, 0.0), hi)


def _sc_proj_body(xn, w_ref, cw_ref, y_ref, carry_ref):
    ts = xn.shape[0]
    wd = MXU_COLS
    for lo in range(0, SC_WIDTH, wd):
        u = _dot(xn, w_ref[:, lo:lo + wd])
        gate_b = _dot(xn, w_ref[:, SC_WIDTH + lo:SC_WIDTH + lo + wd])
        gate_c = _dot(xn, w_ref[:, 2 * SC_WIDTH + lo:2 * SC_WIDTH + lo + wd])
        cu = gate_c * u
        prev = carry_ref[:, lo:lo + wd]
        conv = (_shift_rows(cu, prev, 2) * cw_ref[0:1, lo:lo + wd]
                + _shift_rows(cu, prev, 1) * cw_ref[1:2, lo:lo + wd]
                + cu * cw_ref[2:3, lo:lo + wd])
        y_ref[0, :, lo:lo + wd] = (gate_b * conv).astype(BF16)
        carry_ref[:, lo:lo + wd] = cu[ts - SUBLANES:ts]


def _split_dot(x, m):
    hi = x.astype(BF16)
    lo = (x - hi.astype(F32)).astype(BF16)
    return _dot(hi, m) + _dot(lo, m)


def _rw_proj_body(xn, w_ref, mu_ref, wlora_ref, gup_ref, w0_ref, a0_ref, kk_ref, ka_ref, hsum_ref,
                  r_ref, lw_ref, k_ref, v_ref, kn_ref, a_ref, gg_ref, carry_ref):
    ts = xn.shape[0]

    def mixed(lo, width):
        z = _dot(xn, w_ref[:, lo:lo + width])
        zp = _shift_rows(z, carry_ref[:, lo:lo + width], 1)
        carry_ref[:, lo:lo + width] = z[ts - SUBLANES:ts]
        return z + (zp - z) * mu_ref[:, lo:lo + width]

    lora_lo = 3 * RW_WIDTH
    zlg = mixed(lora_lo, RW_IN - lora_lo)
    zl = zlg[:, :LANES]
    lane = lax.broadcasted_iota(jnp.int32, zl.shape, 1)
    zl = jnp.where(lane < RW_W_LORA, jnp.tanh(zl), zl).astype(BF16)
    zg = jax.nn.sigmoid(zlg[:, LANES:]).astype(BF16)
    wd = MXU_COLS
    for lo in range(0, RW_WIDTH, wd):
        sl = slice(lo, lo + wd)
        up = _dot(zl, wlora_ref[:, lo:lo + wd])
        ua = _dot(zl, wlora_ref[:, RW_WIDTH + lo:RW_WIDTH + lo + wd])
        lw_ref[0, :, sl] = -RW_DECAY_SCALE * jax.nn.sigmoid(w0_ref[:, sl] + up)
        a = jax.nn.sigmoid(a0_ref[:, sl] + ua)
        a_ref[0, :, sl] = a.astype(BF16)
        gg_ref[0, :, sl] = _dot(zg, gup_ref[:, sl]).astype(BF16)
        r_ref[0, :, sl] = mixed(lo, wd).astype(BF16)
        k = mixed(RW_WIDTH + lo, wd)
        v_ref[0, :, sl] = mixed(2 * RW_WIDTH + lo, wd).astype(BF16)
        kk = k * kk_ref[:, sl]
        ss = _split_dot(kk * kk, hsum_ref[...])
        kn_ref[0, :, sl] = (kk * lax.rsqrt(jnp.maximum(ss, 1e-24))).astype(BF16)
        k_ref[0, :, sl] = (k * (1.0 + (a - 1.0) * ka_ref[:, sl])).astype(BF16)


def _gate_proj_body(xn, w_ref, b_ref, o_ref):
    n = w_ref.shape[1]
    step = 4 * LANES
    for lo in range(0, n, step):
        y = _dot(xn, w_ref[:, lo:lo + step]) + b_ref[:, lo:lo + step]
        o_ref[0, :, lo:lo + step] = jax.nn.sigmoid(y).astype(BF16)


N_ATTN_IN, N_SC_IN, N_RW_IN, N_GATE_IN = 7, 2, 9, 2
N_ATTN_OUT, N_SC_OUT, N_RW_OUT = 5, 1, 7


def _in_proj_kernel(*refs):
    h_ref, g_ref = refs[:2]
    pos = 2
    attn_in = refs[pos:pos + N_ATTN_IN]; pos += N_ATTN_IN
    sc_in = refs[pos:pos + N_SC_IN]; pos += N_SC_IN
    rw_in = refs[pos:pos + N_RW_IN]; pos += N_RW_IN
    gate_in = refs[pos:pos + N_GATE_IN]; pos += N_GATE_IN
    attn_out = refs[pos:pos + N_ATTN_OUT]; pos += N_ATTN_OUT
    sc_out = refs[pos:pos + N_SC_OUT]; pos += N_SC_OUT
    rw_out = refs[pos:pos + N_RW_OUT]; pos += N_RW_OUT
    gate_out = refs[pos]; pos += 1
    sc_carry, rw_carry = refs[pos:pos + 2]

    @pl.when(pl.program_id(1) == 0)
    def _():
        sc_carry[...] = jnp.zeros_like(sc_carry)
        rw_carry[...] = jnp.zeros_like(rw_carry)

    xn = _rms(h_ref[0], g_ref[...])
    _attn_proj_body(xn, *attn_in, *attn_out)
    _sc_proj_body(xn, *sc_in, *sc_out, sc_carry)
    _rw_proj_body(xn, *rw_in, *rw_out, rw_carry)
    _gate_proj_body(xn, *gate_in, gate_out)


def _in_proj(h, g, attn_in, sc_in, rw_in, gate_in, ts):
    b, s, d = h.shape
    row = lambda i, j: (i, j, 0)
    const = lambda i, j: (0, 0)

    def seq_spec(width):
        return pl.BlockSpec((1, ts, width), row)

    def whole(x):
        if x.size * x.dtype.itemsize >= (1 << 20):
            return pl.BlockSpec(x.shape, const, pipeline_mode=pl.Buffered(1))
        return pl.BlockSpec(x.shape, const)

    w_attn, tabs = attn_in[0], attn_in[1:]
    in_specs = ([seq_spec(d), whole(g), whole(w_attn)] + [seq_spec(LANES)] * len(tabs)
                + [whole(x) for x in sc_in] + [whole(x) for x in rw_in]
                + [whole(x) for x in gate_in])
    out_widths = ([(A_WIDTH, BF16), (LANES, BF16), (4 * LANES, BF16), (IDX_WIDTH, BF16),
                   (LANES, F32), (SC_WIDTH, BF16)]
                  + [(RW_WIDTH, F32 if i == 1 else BF16) for i in range(N_RW_OUT)]
                  + [(gate_in[0].shape[1], BF16)])
    outs = pl.pallas_call(
        _in_proj_kernel,
        grid=(b, s // ts),
        in_specs=in_specs,
        out_specs=[seq_spec(w) for w, _ in out_widths],
        out_shape=[jax.ShapeDtypeStruct((b, s, w), dt) for w, dt in out_widths],
        scratch_shapes=[pltpu.VMEM((SUBLANES, SC_WIDTH), F32), pltpu.VMEM((SUBLANES, RW_IN), F32)],
        compiler_params=pltpu.CompilerParams(dimension_semantics=("arbitrary", "arbitrary"),
                                             vmem_limit_bytes=WIDE_VMEM_LIMIT),
        name="in_proj",
    )(h, g, *attn_in, *sc_in, *rw_in, *gate_in)
    return outs[:5], outs[5], outs[6:13], outs[13]


def _dsa_topk_bias(qi_ref, kwk_ref, kwq_ref, key_ref, hi_ref, lo_ref, bias_ref, *, t0, k_sel):
    nb, tq_one = qi_ref.shape[0], qi_ref.shape[1]
    tq, sk = bias_ref.shape
    half_min = -(1 << 15)

    for bb in range(nb):
        ki = kwk_ref[bb][:, 0:IDX_DIM].astype(BF16)
        wq = kwq_ref[bb]
        qi = qi_ref[bb]
        isc = jnp.zeros((tq_one, sk), F32)
        for h in range(IDX_HEADS):
            rel = jnp.maximum(_dot_nt(qi[:, h * IDX_DIM:(h + 1) * IDX_DIM], ki), 0.0)
            isc = isc + wq[:, IDX_DIM + h:IDX_DIM + h + 1] * rel

        cols = lax.broadcasted_iota(jnp.int32, (tq_one, sk), 1)
        rows = t0 + lax.broadcasted_iota(jnp.int32, (tq_one, sk), 0)
        bits = pltpu.bitcast(isc, jnp.int32)
        key = bits ^ ((bits >> 31) & jnp.int32(0x7FFFFFFF))
        key = jnp.where(cols <= rows, key, INT_MIN)
        rs = slice(bb * tq_one, (bb + 1) * tq_one)
        key_ref[rs, :] = key
        hi_ref[rs, :] = (key >> 16).astype(jnp.int16)
        lo_ref[rs, :] = ((key & jnp.int32(0xFFFF)) + half_min).astype(jnp.int16)


    def search_half(src_ref, cnt0):
        def body(it, carry):
            tau_u, cnt_tau = carry
            cand_u = tau_u | jnp.left_shift(jnp.int32(1), 15 - it)
            cand = jnp.broadcast_to(cand_u + half_min, (tq, LANES)).astype(jnp.int16)
            accs = [jnp.zeros((tq, LANES), jnp.int16) for _ in range(4)]
            for j in range(sk // LANES):
                hit = src_ref[:, j * LANES:(j + 1) * LANES] >= cand
                accs[j % 4] = accs[j % 4] + jnp.where(hit, jnp.int16(1), jnp.int16(0))
            acc = (accs[0] + accs[1]) + (accs[2] + accs[3])
            cnt = jnp.sum(acc.astype(F32), axis=1, keepdims=True)
            ok = cnt >= k_sel
            return jnp.where(ok, cand_u, tau_u), jnp.where(ok, cnt, cnt_tau)
        return lax.fori_loop(0, 16, body, (jnp.zeros((tq, 1), jnp.int32), cnt0), unroll=True)

    tau_hi_u, cnt_hi = search_half(hi_ref, jnp.full((tq, 1), float(sk), F32))
    tau_hi = tau_hi_u + half_min
    t_hi = jnp.broadcast_to(tau_hi, (tq, LANES)).astype(jnp.int16)
    for j in range(sk // LANES):
        sl = slice(j * LANES, (j + 1) * LANES)
        hi_j = hi_ref[:, sl]
        inside = jnp.where(hi_j == t_hi, lo_ref[:, sl], jnp.int16(half_min))
        lo_ref[:, sl] = jnp.where(hi_j > t_hi, jnp.int16(-half_min - 1), inside)
    tau_lo_u, cnt_ge = search_half(lo_ref, cnt_hi)
    tau = (tau_hi << 16) | tau_lo_u

    drop = jnp.where(tau == INT_MIN, float(2 * sk), cnt_ge - k_sel)
    r2 = lax.broadcasted_iota(jnp.int32, (LANES, 2 * LANES), 0)
    c2 = lax.broadcasted_iota(jnp.int32, (LANES, 2 * LANES), 1)
    tri = jnp.where((r2 > c2) | (c2 >= LANES), 1.0, 0.0).astype(BF16)
    after = jnp.zeros((tq, LANES), F32)
    for j in reversed(range(sk // LANES)):
        sl = slice(j * LANES, (j + 1) * LANES)
        key_j = key_ref[:, sl]
        eq = key_j == tau
        pr = _dot(jnp.where(eq, 1.0, 0.0).astype(BF16), tri)
        tied = jnp.where(eq, jnp.where(after + pr[:, :LANES] >= drop, 0.0, -jnp.inf), -jnp.inf)
        bias_ref[:, sl] = jnp.where(key_j > tau, 0.0, tied)
        after = after + pr[:, LANES:]


def _dsa_kernel(q_ref, k_ref, v_ref, qi_ref, kwk_ref, kwq_ref, o_ref,
                key_ref, hi_ref, lo_ref, bias_ref, *, first_block, k_sel):
    nb, tq = q_ref.shape[0], q_ref.shape[1]
    sk = k_ref.shape[1]
    t0 = (first_block + pl.program_id(1)) * tq
    if sk <= k_sel:
        cols = lax.broadcasted_iota(jnp.int32, (tq, sk), 1)
        rows = t0 + lax.broadcasted_iota(jnp.int32, (tq, sk), 0)
        for bb in range(nb):
            bias_ref[bb * tq:(bb + 1) * tq, :] = jnp.where(cols <= rows, 0.0, -jnp.inf)
    else:
        _dsa_topk_bias(qi_ref, kwk_ref, kwq_ref, key_ref, hi_ref, lo_ref, bias_ref,
                       t0=t0, k_sel=k_sel)

    group = A_HEADS // A_KV_HEADS
    low = lax.broadcasted_iota(jnp.int32, (tq, LANES), 1) < A_HEAD_DIM
    for bb in range(nb):
        bias = bias_ref[bb * tq:(bb + 1) * tq, :]
        kk = k_ref[bb]
        for h2 in range(A_HEADS // 2):
            res = []
            for h in (2 * h2, 2 * h2 + 1):
                n = h // group
                kn = kk[:, n * A_HEAD_DIM:(n + 1) * A_HEAD_DIM]
                s = _dot_nt(q_ref[bb, :, h * A_HEAD_DIM:(h + 1) * A_HEAD_DIM], kn) + bias
                m = jnp.max(s, axis=1, keepdims=True)
                p = jnp.exp(s - m).astype(BF16)
                vx = v_ref[bb, :, (2 * n + h % 2) * LANES:(2 * n + h % 2 + 1) * LANES]
                oe = _dot(p, vx)
                res.append(oe / pltpu.roll(oe, A_HEAD_DIM, 1))
            o_ref[bb, :, h2 * LANES:(h2 + 1) * LANES] = jnp.where(low, res[0],
                                                                   res[1]).astype(BF16)


def _dsa(q, k, v, qi, kw, k_sel):
    b, s, _ = q.shape
    tq = min(DSA_Q_TILE, s)
    nblk = s // tq
    per = max(1, nblk // DSA_CLASSES)
    nb = DSA_BATCH_PER_STEP if b % DSA_BATCH_PER_STEP == 0 else 1
    outs = []
    for first in range(0, nblk, per):
        sk = (first + per) * tq
        qmap = lambda i, j, first=first: (i, first + j, 0)
        kmap = lambda i, j: (i, 0, 0)
        outs.append(pl.pallas_call(
            functools.partial(_dsa_kernel, first_block=first, k_sel=k_sel),
            grid=(b // nb, per),
            in_specs=[pl.BlockSpec((nb, tq, A_WIDTH), qmap), pl.BlockSpec((nb, sk, LANES), kmap),
                      pl.BlockSpec((nb, sk, 4 * LANES), kmap),
                      pl.BlockSpec((nb, tq, IDX_WIDTH), qmap),
                      pl.BlockSpec((nb, sk, LANES), kmap), pl.BlockSpec((nb, tq, LANES), qmap)],
            out_specs=pl.BlockSpec((nb, tq, A_WIDTH), lambda i, j: (i, j, 0)),
            out_shape=jax.ShapeDtypeStruct((b, per * tq, A_WIDTH), BF16),
            scratch_shapes=[pltpu.VMEM((nb * tq, sk), jnp.int32),
                            pltpu.VMEM((nb * tq, sk), jnp.int16),
                            pltpu.VMEM((nb * tq, sk), jnp.int16), pltpu.VMEM((nb * tq, sk), F32)],
            compiler_params=_cparams(("arbitrary", "arbitrary")),
            name=f"dsa_{sk}",
        )(q, k, v, qi, kw, kw))
    return jnp.concatenate(outs, axis=1)


def _rwkv_kernel(r_ref, lw_ref, k_ref, v_ref, kn_ref, a_ref, g_ref, rk_ref, lnw_ref, lnb_ref,
                 y_ref, st_ref, *, c):
    @pl.when(pl.program_id(1) == 0)
    def _():
        st_ref[...] = jnp.zeros_like(st_ref)

    width = r_ref.shape[2]
    nc = r_ref.shape[1] // c
    hd = RW_HEAD_DIM

    lw_all = lw_ref[0]
    rows = lax.broadcasted_iota(jnp.int32, lw_all.shape, 0) & (c - 1)
    cum_all = lw_all
    step = 1
    while step < c:
        cum_all = cum_all + jnp.where(rows >= step, pltpu.roll(cum_all, step, 0), 0.0)
        step *= 2

    lane = lax.broadcasted_iota(jnp.int32, (c, LANES), 1)
    first = lane < hd
    r2 = lax.broadcasted_iota(jnp.int32, (2 * c, 2 * c), 0)
    c2 = lax.broadcasted_iota(jnp.int32, (2 * c, 2 * c), 1)
    same_head = (r2 >= c) == (c2 >= c)
    t_row = r2 & (c - 1)
    t_col = c2 & (c - 1)
    strict = same_head & (t_col < t_row)
    lower = same_head & (t_col <= t_row)

    def blk(x):
        return jnp.concatenate([jnp.where(first, x, 0.0), jnp.where(first, 0.0, x)],
                               axis=0).astype(BF16)

    def head_sum(x):
        s0 = jnp.sum(jnp.where(first, x, 0.0), axis=1, keepdims=True)
        s1 = jnp.sum(jnp.where(first, 0.0, x), axis=1, keepdims=True)
        return jnp.where(first, s0, s1)

    pairs = range(width // LANES)
    sls = [slice(p * LANES, (p + 1) * LANES) for p in pairs]
    units = [(j, p) for j in range(nc) for p in pairs]
    n = 2 * c
    la, lr, vb, bk_t, bk_h, p_last, vals, bonus_in = {}, {}, {}, {}, {}, {}, {}, {}
    for j in range(nc):
        rs = slice(j * c, (j + 1) * c)
        r, kf, v = (x[0, rs, :].astype(F32) for x in (r_ref, k_ref, v_ref))
        kn, a = kn_ref[0, rs, :].astype(F32), a_ref[0, rs, :].astype(F32)
        lw, cum = lw_all[rs], cum_all[rs]
        cum_last = cum[c - 1:c, :]
        p_inv = jnp.exp(-cum)
        p_rem = jnp.exp(cum_last - cum)
        kb = kn * a
        a_t = -kn * jnp.exp(cum - lw)
        r_t = r * jnp.exp(cum)
        b_t = kb * p_inv
        k_t = kf * p_inv
        b_h = kb * p_rem
        k_h = kf * p_rem
        p_last[j] = jnp.exp(cum_last)
        vals[j] = v
        bonus_in[j] = r * kf * rk_ref[...]
        for p in pairs:
            sl = sls[p]
            la[j, p], lr[j, p], vb[j, p] = blk(a_t[:, sl]), blk(r_t[:, sl]), blk(v[:, sl])
            bk_t[j, p] = jnp.concatenate([blk(b_t[:, sl]), blk(k_t[:, sl])], axis=0)
            bk_h[j, p] = jnp.concatenate([blk(b_h[:, sl]), blk(k_h[:, sl])], axis=0)
    gram = {u: _dot_nt(jnp.concatenate([la[u], lr[u]], axis=0), bk_t[u]) for u in units}
    x = {u: jnp.where(strict, gram[u][:n, :n], 0.0) for u in units}
    npow = dict(x)
    for _ in range(int(math.log2(c)) - 1):
        npow_b = {u: npow[u].astype(BF16) for u in units}
        npow = {u: _dot(npow_b[u], npow_b[u]) for u in units}
        x = {u: x[u] + npow[u] + _dot(x[u].astype(BF16), npow[u].astype(BF16)) for u in units}
    x_b = {u: x[u].astype(BF16) for u in units}
    a_ak = {u: jnp.where(strict, gram[u][:n, n:], 0.0).astype(BF16) for u in units}
    lower2 = jnp.concatenate([lower, lower], axis=1)
    a_rbk = {u: jnp.where(lower2, gram[u][n:, :], 0.0).astype(BF16) for u in units}
    akv = {u: _dot(a_ak[u], vb[u]) for u in units}

    st = [st_ref[p] for p in pairs]
    for j in range(nc):
        st_b = [s.astype(BF16) for s in st]
        rhs = [_dot_nt(la[j, p], st_b[p]) + akv[j, p] for p in pairs]
        sa = [rhs[p] + _dot(x_b[j, p], rhs[p].astype(BF16)) for p in pairs]
        sav = [jnp.concatenate([sa[p].astype(BF16), vb[j, p]], axis=0) for p in pairs]
        yb = [_dot_nt(lr[j, p], st_b[p]) + _dot(a_rbk[j, p], sav[p]) for p in pairs]
        st = [st[p] * p_last[j][:, sls[p]] + _dot_tn(sav[p], bk_h[j, p]) for p in pairs]
        rs = slice(j * c, (j + 1) * c)
        for p in pairs:
            sl = sls[p]
            y = yb[p][:c] + yb[p][c:]
            mean = head_sum(y) * (1.0 / hd)
            yc = y - mean
            var = head_sum(yc * yc) * (1.0 / hd)
            yn = yc * lax.rsqrt(var + RW_GN_EPS) * lnw_ref[:, sl] + lnb_ref[:, sl]
            bonus = head_sum(bonus_in[j][:, sl]) * vals[j][:, sl]
            y_ref[0, rs, sl] = ((yn + bonus) * g_ref[0, rs, sl]).astype(BF16)
    for p in pairs:
        st_ref[p] = st[p]


def _rwkv(r, lw, kf, v, kn, a, g, r_k, ln_w, ln_b):
    b, s, w = r.shape
    c = min(RW_CHUNK, s)
    rows = min(RW_CHUNKS_PER_STEP * c, s)
    row = lambda i, j: (i, j, 0)
    const = lambda i, j: (0, 0)
    seq = pl.BlockSpec((1, rows, w), row)
    vec = pl.BlockSpec((1, w), const)
    return pl.pallas_call(
        functools.partial(_rwkv_kernel, c=c),
        grid=(b, s // rows),
        in_specs=[seq] * 7 + [vec] * 3,
        out_specs=seq,
        out_shape=jax.ShapeDtypeStruct((b, s, w), BF16),
        scratch_shapes=[pltpu.VMEM((w // LANES, LANES, LANES), F32)],
        compiler_params=_cparams(("arbitrary", "arbitrary")),
        name="rwkv7",
    )(r, lw, kf, v, kn, a, g, r_k, ln_w, ln_b)


def _merge_ffn_kernel(h_ref, ya_ref, yb_ref, yc_ref, gt_ref, wb_ref, wo_ref,
                      g_ref, wup_ref, cw_ref, wd_ref, gf_ref, o_ref, act_ref, carry_ref, *,
                      final_norm):
    @pl.when(pl.program_id(1) == 0)
    def _():
        carry_ref[...] = jnp.zeros_like(carry_ref)

    ts, d = h_ref.shape[1], h_ref.shape[2]
    dff = wd_ref.shape[0]
    mixed = None
    for i, y_ref in enumerate((ya_ref, yb_ref, yc_ref)):
        up = _dot(y_ref[0], wb_ref[i])
        term = gt_ref[0, :, i * d:(i + 1) * d].astype(F32) * up
        mixed = term if mixed is None else mixed + term
    h_mid = h_ref[0] + _dot(mixed.astype(BF16), wo_ref[...])
    xn = _rms(h_mid, g_ref[...])

    def conv(sl):
        u = _dot(xn, wup_ref[:, sl])
        prev = carry_ref[:, sl]
        out = (_shift_rows(u, prev, 2) * cw_ref[0:1, sl] + _shift_rows(u, prev, 1) * cw_ref[1:2, sl]
               + u * cw_ref[2:3, sl])
        carry_ref[:, sl] = u[ts - SUBLANES:ts]
        return out

    for lo in range(0, dff, MXU_COLS):
        gate = conv(slice(lo, lo + MXU_COLS))
        up = conv(slice(dff + lo, dff + lo + MXU_COLS))
        act_ref[:, lo:lo + MXU_COLS] = (gate * jax.nn.sigmoid(gate) * up).astype(BF16)
    out = h_mid + _dot(act_ref[...], wd_ref[...])
    if final_norm:
        ms = jnp.mean(out * out, axis=-1, keepdims=True)
        out = out * lax.rsqrt(ms + RMS_EPS) * gf_ref[...]
    o_ref[0] = out


def _merge_ffn(h, ya, yb, yc, gates, wb, wo, g, w_up, w_conv, w_down, g_final, final_norm, ts):
    b, s, d = h.shape
    dff = w_down.shape[0]
    bw = ya.shape[2]
    row = lambda i, j: (i, j, 0)
    const = lambda i, j: (0, 0)
    resident = dict(pipeline_mode=pl.Buffered(1))
    return pl.pallas_call(
        functools.partial(_merge_ffn_kernel, final_norm=final_norm),
        grid=(b, s // ts),
        in_specs=[pl.BlockSpec((1, ts, d), row)] + [pl.BlockSpec((1, ts, bw), row)] * 3
                 + [pl.BlockSpec((1, ts, N_BRANCH * d), row),
                    pl.BlockSpec((N_BRANCH, bw, d), lambda i, j: (0, 0, 0), **resident),
                    pl.BlockSpec((d, d), const, **resident),
                    pl.BlockSpec((1, d), const),
                    pl.BlockSpec((d, 2 * dff), const, **resident),
                    pl.BlockSpec((3, 2 * dff), const),
                    pl.BlockSpec((dff, d), const, **resident),
                    pl.BlockSpec((1, d), const)],
        out_specs=pl.BlockSpec((1, ts, d), row),
        out_shape=jax.ShapeDtypeStruct((b, s, d), F32),
        scratch_shapes=[pltpu.VMEM((ts, dff), BF16), pltpu.VMEM((SUBLANES, 2 * dff), F32)],
        compiler_params=pltpu.CompilerParams(dimension_semantics=("arbitrary", "arbitrary"),
                                             vmem_limit_bytes=WIDE_VMEM_LIMIT),
        name="merge_ffn",
    )(h, ya, yb, yc, gates, wb, wo, g, w_up, w_conv, w_down, g_final)


def _rope_tables(positions, head_dim):
    rot = head_dim // ROPE_FRACTION
    half = rot // 2
    inv = ROPE_THETA ** (-jnp.arange(half, dtype=F32) * 2.0 / rot)
    ang = positions.astype(F32)[:, :, None] * inv
    cos, sin = jnp.cos(ang), jnp.sin(ang)
    zeros = jnp.zeros_like(cos)
    rest = head_dim - rot
    pad1 = jnp.ones(cos.shape[:2] + (rest,), F32)
    pad0 = jnp.zeros(cos.shape[:2] + (rest,), F32)
    reps = LANES // head_dim
    c = jnp.tile(jnp.concatenate([cos, cos, pad1], axis=-1), (1, 1, reps))
    s1 = jnp.tile(jnp.concatenate([-sin, zeros, pad0], axis=-1), (1, 1, reps))
    s2 = jnp.tile(jnp.concatenate([zeros, sin, pad0], axis=-1), (1, 1, reps))
    return c, s1, s2


def kernel(x, positions, norm_mix, w_in, b_gate, sc_conv, rw_mu, rw_w0, rw_w_up, rw_a0, rw_a_up, rw_g_up, rw_k_k, rw_k_a, rw_r_k, rw_ln_w, rw_ln_b, w_branch, w_out, norm_ffn, ffn_up, ffn_conv, ffn_down, norm_final):
    bsz, seq, d = x.shape
    depth = w_in.shape[0]
    ts = min(ROW_TILE, seq)
    k_sel = min(TOPK_MAX, seq // 4)

    tabs = _rope_tables(positions, A_HEAD_DIM) + _rope_tables(positions, IDX_DIM)
    o_q = 0
    o_k = o_q + A_WIDTH
    o_v = o_k + A_KV_WIDTH
    o_qi = o_v + A_KV_WIDTH
    o_ki = o_qi + IDX_WIDTH
    o_wi = o_ki + IDX_DIM
    o_sc = o_wi + IDX_HEADS
    o_rw = o_sc + 3 * SC_WIDTH
    o_gate = o_rw + RW_IN
    attn_scale = A_HEAD_DIM ** -0.5
    idx_scale = (IDX_HEADS ** -0.5) * (IDX_DIM ** -0.5)
    head_sum = jnp.kron(jnp.eye(MXU_COLS // RW_HEAD_DIM, dtype=F32),
                        jnp.ones((RW_HEAD_DIM, RW_HEAD_DIM), F32)).astype(BF16)

    h = x
    for l in range(depth):
        w = w_in[l]
        pad = jnp.zeros((d, LANES - IDX_DIM - IDX_HEADS), F32)
        w_attn = jnp.concatenate(
            [w[:, o_q:o_k] * attn_scale, w[:, o_k:o_ki], w[:, o_ki:o_wi],
             w[:, o_wi:o_sc] * idx_scale, pad], axis=1).astype(BF16)
        w_sc = w[:, o_sc:o_rw].astype(BF16)
        w_rw = w[:, o_rw:o_gate].astype(BF16)
        w_gate = w[:, o_gate:].astype(BF16)
        g_mix = norm_mix[l][None, :]

        zero_lora = jnp.zeros((RW_W_LORA, RW_WIDTH), F32)
        w_lora = jnp.concatenate(
            [jnp.concatenate([rw_w_up[l], zero_lora], axis=1),
             jnp.concatenate([zero_lora, rw_a_up[l]], axis=1)], axis=0).astype(BF16)
        (q, k, v, qi, kw), y_b, (r, lw, kf, vv, kn, a, gg), gates = _in_proj(
            h, g_mix,
            (w_attn,) + tabs,
            (w_sc, sc_conv[l]),
            (w_rw, rw_mu[l][None, :], w_lora, rw_g_up[l].astype(BF16), rw_w0[l][None, :],
             rw_a0[l][None, :], rw_k_k[l][None, :], rw_k_a[l][None, :], head_sum),
            (w_gate, b_gate[l][None, :]), ts)

        y_a = _dsa(q, k, v, qi, kw, k_sel)
        y_c = _rwkv(r, lw, kf, vv, kn, a, gg, rw_r_k[l].reshape(1, RW_WIDTH),
                    rw_ln_w[l][None, :], rw_ln_b[l][None, :])

        h = _merge_ffn(h, y_a, y_b, y_c, gates, w_branch[l].astype(BF16), w_out[l].astype(BF16),
                       norm_ffn[l][None, :], ffn_up[l].astype(BF16), ffn_conv[l],
                       ffn_down[l].astype(BF16), norm_final[None, :], l == depth - 1, ts)
    return h
```

```python
import functools
import math

import jax
import jax.numpy as jnp
from jax import lax
from jax.experimental import pallas as pl
from jax.experimental.pallas import tpu as pltpu

F32 = jnp.float32
BF16 = jnp.bfloat16
HIGHEST = lax.Precision.HIGHEST

D_MODEL = 1024
A_HEADS = 8
A_KV_HEADS = 2
A_HEAD_DIM = 64
A_WIDTH = A_HEADS * A_HEAD_DIM
A_KV_WIDTH = A_KV_HEADS * A_HEAD_DIM
IDX_HEADS = 8
IDX_DIM = 32
IDX_WIDTH = IDX_HEADS * IDX_DIM
TOPK_MAX = 256
ROPE_THETA = 500000.0
ROPE_FRACTION = 4
SC_WIDTH = 512
RW_HEADS = 8
RW_HEAD_DIM = 64
RW_WIDTH = RW_HEADS * RW_HEAD_DIM
RW_W_LORA = 64
RW_A_LORA = 64
RW_G_LORA = 128
RW_IN = 3 * RW_WIDTH + RW_W_LORA + RW_A_LORA + RW_G_LORA
RW_GN_EPS = 64e-5
RW_DECAY_SCALE = math.exp(-0.5)
N_BRANCH = 3
D_FF = 2816
RMS_EPS = 1e-6

LANES = 128
MXU_COLS = 256
SUBLANES = 8
INT_MIN = -2147483648
VMEM_LIMIT = 48 * 1024 * 1024
WIDE_VMEM_LIMIT = 56 * 1024 * 1024

ROW_TILE = 512
DSA_Q_TILE = 256
RW_CHUNK = 64


def _cparams(sem):
    return pltpu.CompilerParams(dimension_semantics=sem, vmem_limit_bytes=VMEM_LIMIT)


def _rms(x, g):
    ms = jnp.mean(x * x, axis=-1, keepdims=True)
    return (x * lax.rsqrt(ms + RMS_EPS) * g).astype(BF16)


def _dot(a, b, precision=None):
    return jnp.dot(a, b, preferred_element_type=F32, precision=precision)


def _dot_nt(a, b, precision=None):
    return lax.dot_general(a, b, (((1,), (1,)), ((), ())), preferred_element_type=F32,
                           precision=precision)


def _dot_tn(a, b, precision=None):
    return lax.dot_general(a, b, (((0,), (0,)), ((), ())), preferred_element_type=F32,
                           precision=precision)


def _shift_rows(x, prev, k):
    xr = pltpu.roll(x, k, 0)
    pr = pltpu.roll(prev, k, 0)
    rows = lax.broadcasted_iota(jnp.int32, pr.shape, 0)
    head = jnp.where(rows < k, pr, xr[0:SUBLANES])
    return jnp.concatenate([head, xr[SUBLANES:]], axis=0)


def _rope_slab(y, c, s1, s2, half):
    return y * c + pltpu.roll(y, LANES - half, 1) * s1 + pltpu.roll(y, half, 1) * s2


def _attn_proj_body(xn, w_ref, cq_ref, s1q_ref, s2q_ref, ci_ref, s1i_ref, s2i_ref,
                    q_ref, k_ref, v_ref, qi_ref, kw_ref):
    cq, s1q, s2q = cq_ref[0], s1q_ref[0], s2q_ref[0]
    ci, s1i, s2i = ci_ref[0], s1i_ref[0], s2i_ref[0]
    hq = A_HEAD_DIM // ROPE_FRACTION // 2
    hi = IDX_DIM // ROPE_FRACTION // 2
    col = 0
    for p in range(A_WIDTH // MXU_COLS):
        y = _dot(xn, w_ref[:, col:col + MXU_COLS])
        for half in range(MXU_COLS // LANES):
            lo = p * MXU_COLS + half * LANES
            q_ref[0, :, lo:lo + LANES] = _rope_slab(
                y[:, half * LANES:(half + 1) * LANES], cq, s1q, s2q, hq).astype(BF16)
        col += MXU_COLS
    y = _dot(xn, w_ref[:, col:col + 2 * LANES])
    k_ref[0] = _rope_slab(y[:, :LANES], cq, s1q, s2q, hq).astype(BF16)
    val = y[:, LANES:]
    swapped = pltpu.roll(val, A_HEAD_DIM, 1)
    low = lax.broadcasted_iota(jnp.int32, val.shape, 1) < A_HEAD_DIM
    slabs = (jnp.where(low, val, 1.0), jnp.where(low, 1.0, swapped),
             jnp.where(low, swapped, 1.0), jnp.where(low, 1.0, val))
    for i, slab in enumerate(slabs):
        v_ref[0, :, i * LANES:(i + 1) * LANES] = slab.astype(BF16)
    col += 2 * LANES
    y = _dot(xn, w_ref[:, col:col + IDX_WIDTH])
    for half in range(IDX_WIDTH // LANES):
        qi_ref[0, :, half * LANES:(half + 1) * LANES] = _rope_slab(
            y[:, half * LANES:(half + 1) * LANES], ci, s1i, s2i, hi).astype(BF16)
    col += IDX_WIDTH
    y = _dot(xn, w_ref[:, col:col + LANES])
    lane = lax.broadcasted_iota(jnp.int32, y.shape, 1)
    is_key = lane < IDX_DIM
    ck = jnp.where(is_key, ci, 1.0)
    kw_ref[0] = _rope_slab(y, ck, jnp.where(is_key, s1i, 0.0), jnp.where(is_key, s2i, 0.0), hi)


def _sc_proj_body(xn, w_ref, cw_ref, y_ref, carry_ref):
    ts = xn.shape[0]
    wd = MXU_COLS
    for lo in range(0, SC_WIDTH, wd):
        u = _dot(xn, w_ref[:, lo:lo + wd])
        gate_b = _dot(xn, w_ref[:, SC_WIDTH + lo:SC_WIDTH + lo + wd])
        gate_c = _dot(xn, w_ref[:, 2 * SC_WIDTH + lo:2 * SC_WIDTH + lo + wd])
        cu = gate_c * u
        prev = carry_ref[:, lo:lo + wd]
        conv = (_shift_rows(cu, prev, 2) * cw_ref[0:1, lo:lo + wd]
                + _shift_rows(cu, prev, 1) * cw_ref[1:2, lo:lo + wd]
                + cu * cw_ref[2:3, lo:lo + wd])
        y_ref[0, :, lo:lo + wd] = (gate_b * conv).astype(BF16)
        carry_ref[:, lo:lo + wd] = cu[ts - SUBLANES:ts]


def _split_dot(x, m):
    hi = x.astype(BF16)
    lo = (x - hi.astype(F32)).astype(BF16)
    return _dot(hi, m) + _dot(lo, m)


def _rw_proj_body(xn, w_ref, mu_ref, wlora_ref, gup_ref, w0_ref, a0_ref, kk_ref, ka_ref, hsum_ref,
                  r_ref, lw_ref, k_ref, v_ref, kn_ref, a_ref, gg_ref, carry_ref):
    ts = xn.shape[0]

    def mixed(lo, width):
        z = _dot(xn, w_ref[:, lo:lo + width])
        zp = _shift_rows(z, carry_ref[:, lo:lo + width], 1)
        carry_ref[:, lo:lo + width] = z[ts - SUBLANES:ts]
        return z + (zp - z) * mu_ref[:, lo:lo + width]

    lora_lo = 3 * RW_WIDTH
    zlg = mixed(lora_lo, RW_IN - lora_lo)
    zl = zlg[:, :LANES]
    lane = lax.broadcasted_iota(jnp.int32, zl.shape, 1)
    zl = jnp.where(lane < RW_W_LORA, jnp.tanh(zl), zl).astype(BF16)
    zg = jax.nn.sigmoid(zlg[:, LANES:]).astype(BF16)
    wd = MXU_COLS
    for lo in range(0, RW_WIDTH, wd):
        sl = slice(lo, lo + wd)
        up = _dot(zl, wlora_ref[:, lo:lo + wd])
        ua = _dot(zl, wlora_ref[:, RW_WIDTH + lo:RW_WIDTH + lo + wd])
        lw_ref[0, :, sl] = -RW_DECAY_SCALE * jax.nn.sigmoid(w0_ref[:, sl] + up)
        a = jax.nn.sigmoid(a0_ref[:, sl] + ua)
        a_ref[0, :, sl] = a.astype(BF16)
        gg_ref[0, :, sl] = _dot(zg, gup_ref[:, sl]).astype(BF16)
        r_ref[0, :, sl] = mixed(lo, wd).astype(BF16)
        k = mixed(RW_WIDTH + lo, wd)
        v_ref[0, :, sl] = mixed(2 * RW_WIDTH + lo, wd).astype(BF16)
        kk = k * kk_ref[:, sl]
        ss = _split_dot(kk * kk, hsum_ref[...])
        kn_ref[0, :, sl] = (kk * lax.rsqrt(jnp.maximum(ss, 1e-24))).astype(BF16)
        k_ref[0, :, sl] = (k * (1.0 + (a - 1.0) * ka_ref[:, sl])).astype(BF16)


def _gate_proj_body(xn, w_ref, b_ref, o_ref):
    n = w_ref.shape[1]
    step = 4 * LANES
    for lo in range(0, n, step):
        y = _dot(xn, w_ref[:, lo:lo + step]) + b_ref[:, lo:lo + step]
        o_ref[0, :, lo:lo + step] = jax.nn.sigmoid(y).astype(BF16)


N_ATTN_IN, N_SC_IN, N_RW_IN, N_GATE_IN = 7, 2, 9, 2
N_ATTN_OUT, N_SC_OUT, N_RW_OUT = 5, 1, 7


def _in_proj_kernel(*refs):
    h_ref, g_ref = refs[:2]
    pos = 2
    attn_in = refs[pos:pos + N_ATTN_IN]; pos += N_ATTN_IN
    sc_in = refs[pos:pos + N_SC_IN]; pos += N_SC_IN
    rw_in = refs[pos:pos + N_RW_IN]; pos += N_RW_IN
    gate_in = refs[pos:pos + N_GATE_IN]; pos += N_GATE_IN
    attn_out = refs[pos:pos + N_ATTN_OUT]; pos += N_ATTN_OUT
    sc_out = refs[pos:pos + N_SC_OUT]; pos += N_SC_OUT
    rw_out = refs[pos:pos + N_RW_OUT]; pos += N_RW_OUT
    gate_out = refs[pos]; pos += 1
    sc_carry, rw_carry = refs[pos:pos + 2]

    @pl.when(pl.program_id(1) == 0)
    def _():
        sc_carry[...] = jnp.zeros_like(sc_carry)
        rw_carry[...] = jnp.zeros_like(rw_carry)

    xn = _rms(h_ref[0], g_ref[...])
    _attn_proj_body(xn, *attn_in, *attn_out)
    _sc_proj_body(xn, *sc_in, *sc_out, sc_carry)
    _rw_proj_body(xn, *rw_in, *rw_out, rw_carry)
    _gate_proj_body(xn, *gate_in, gate_out)


def _in_proj(h, g, attn_in, sc_in, rw_in, gate_in, ts):
    b, s, d = h.shape
    row = lambda i, j: (i, j, 0)
    const = lambda i, j: (0, 0)

    def seq_spec(width):
        return pl.BlockSpec((1, ts, width), row)

    def whole(x):
        if x.size * x.dtype.itemsize >= (1 << 20):
            return pl.BlockSpec(x.shape, const, pipeline_mode=pl.Buffered(1))
        return pl.BlockSpec(x.shape, const)

    w_attn, tabs = attn_in[0], attn_in[1:]
    in_specs = ([seq_spec(d), whole(g), whole(w_attn)] + [seq_spec(LANES)] * len(tabs)
                + [whole(x) for x in sc_in] + [whole(x) for x in rw_in]
                + [whole(x) for x in gate_in])
    out_widths = ([(A_WIDTH, BF16), (LANES, BF16), (4 * LANES, BF16), (IDX_WIDTH, BF16),
                   (LANES, F32), (SC_WIDTH, BF16)]
                  + [(RW_WIDTH, F32 if i == 1 else BF16) for i in range(N_RW_OUT)]
                  + [(gate_in[0].shape[1], BF16)])
    outs = pl.pallas_call(
        _in_proj_kernel,
        grid=(b, s // ts),
        in_specs=in_specs,
        out_specs=[seq_spec(w) for w, _ in out_widths],
        out_shape=[jax.ShapeDtypeStruct((b, s, w), dt) for w, dt in out_widths],
        scratch_shapes=[pltpu.VMEM((SUBLANES, SC_WIDTH), F32), pltpu.VMEM((SUBLANES, RW_IN), F32)],
        compiler_params=pltpu.CompilerParams(dimension_semantics=("arbitrary", "arbitrary"),
                                             vmem_limit_bytes=WIDE_VMEM_LIMIT),
        name="in_proj",
    )(h, g, *attn_in, *sc_in, *rw_in, *gate_in)
    return outs[:5], outs[5], outs[6:13], outs[13]


def _dsa_topk_bias(qi_ref, kwk_ref, kwq_ref, key_ref, hi_ref, lo_ref, bias_ref, *, t0, k_sel,
                   side_work):
    nb, tq_one = qi_ref.shape[0], qi_ref.shape[1]
    tq, sk = bias_ref.shape
    half_min = -(1 << 15)

    for bb in range(nb):
        ki = kwk_ref[bb][:, 0:IDX_DIM].astype(BF16)
        wq = kwq_ref[bb]
        qi = qi_ref[bb]
        isc = jnp.zeros((tq_one, sk), F32)
        for h in range(IDX_HEADS):
            rel = jnp.maximum(_dot_nt(qi[:, h * IDX_DIM:(h + 1) * IDX_DIM], ki), 0.0)
            isc = isc + wq[:, IDX_DIM + h:IDX_DIM + h + 1] * rel

        cols = lax.broadcasted_iota(jnp.int32, (tq_one, sk), 1)
        rows = t0 + lax.broadcasted_iota(jnp.int32, (tq_one, sk), 0)
        bits = pltpu.bitcast(isc, jnp.int32)
        key = bits ^ ((bits >> 31) & jnp.int32(0x7FFFFFFF))
        key = jnp.where(cols <= rows, key, INT_MIN)
        rs = slice(bb * tq_one, (bb + 1) * tq_one)
        key_ref[rs, :] = key
        hi_ref[rs, :] = (key >> 16).astype(jnp.int16)
        lo_ref[rs, :] = ((key & jnp.int32(0xFFFF)) + half_min).astype(jnp.int16)


    def search_half(src_ref, cnt0):
        tau_u, cnt_tau = jnp.zeros((tq, 1), jnp.int32), cnt0
        for it in range(16):
            cand_u = tau_u | (1 << (15 - it))
            cand = jnp.broadcast_to(cand_u + half_min, (tq, LANES)).astype(jnp.int16)
            accs = [jnp.zeros((tq, LANES), jnp.int16) for _ in range(4)]
            for j in range(sk // LANES):
                hit = src_ref[:, j * LANES:(j + 1) * LANES] >= cand
                accs[j % 4] = accs[j % 4] + jnp.where(hit, jnp.int16(1), jnp.int16(0))
            acc = (accs[0] + accs[1]) + (accs[2] + accs[3])
            cnt = jnp.sum(acc.astype(F32), axis=1, keepdims=True)
            ok = cnt >= k_sel
            tau_u, cnt_tau = jnp.where(ok, cand_u, tau_u), jnp.where(ok, cnt, cnt_tau)
            next(side_work, None)
        return tau_u, cnt_tau

    tau_hi_u, cnt_hi = search_half(hi_ref, jnp.full((tq, 1), float(sk), F32))
    tau_hi = tau_hi_u + half_min
    t_hi = jnp.broadcast_to(tau_hi, (tq, LANES)).astype(jnp.int16)
    for j in range(sk // LANES):
        sl = slice(j * LANES, (j + 1) * LANES)
        hi_j = hi_ref[:, sl]
        inside = jnp.where(hi_j == t_hi, lo_ref[:, sl], jnp.int16(half_min))
        lo_ref[:, sl] = jnp.where(hi_j > t_hi, jnp.int16(-half_min - 1), inside)
    tau_lo_u, cnt_ge = search_half(lo_ref, cnt_hi)
    tau = (tau_hi << 16) | tau_lo_u

    drop = jnp.where(tau == INT_MIN, float(2 * sk), cnt_ge - k_sel)
    r2 = lax.broadcasted_iota(jnp.int32, (LANES, 2 * LANES), 0)
    c2 = lax.broadcasted_iota(jnp.int32, (LANES, 2 * LANES), 1)
    tri = jnp.where((r2 > c2) | (c2 >= LANES), 1.0, 0.0).astype(BF16)
    after = jnp.zeros((tq, LANES), F32)
    for j in reversed(range(sk // LANES)):
        sl = slice(j * LANES, (j + 1) * LANES)
        key_j = key_ref[:, sl]
        eq = key_j == tau
        pr = _dot(jnp.where(eq, 1.0, 0.0).astype(BF16), tri)
        tied = jnp.where(eq, jnp.where(after + pr[:, :LANES] >= drop, 0.0, -jnp.inf), -jnp.inf)
        bias_ref[:, sl] = jnp.where(key_j > tau, 0.0, tied)
        after = after + pr[:, LANES:]


N_RW_STREAMS = 7
N_RW_REFS = N_RW_STREAMS + 4


def _dsa_rwkv_kernel(*refs, first_block, k_sel, rw_chunk):
    q_ref, k_ref, v_ref, qi_ref, kwk_ref, kwq_ref = refs[:6]
    rw_in = refs[6:6 + N_RW_REFS]
    o_ref, y_ref, st_out_ref, key_ref, hi_ref, lo_ref, bias_ref = refs[6 + N_RW_REFS:]
    rwkv = _rwkv_stages(*rw_in, y_ref, st_out_ref, c=rw_chunk)
    nb, tq = q_ref.shape[0], q_ref.shape[1]
    sk = k_ref.shape[1]
    t0 = (first_block + pl.program_id(1)) * tq
    if sk <= k_sel:
        cols = lax.broadcasted_iota(jnp.int32, (tq, sk), 1)
        rows = t0 + lax.broadcasted_iota(jnp.int32, (tq, sk), 0)
        for bb in range(nb):
            bias_ref[bb * tq:(bb + 1) * tq, :] = jnp.where(cols <= rows, 0.0, -jnp.inf)
    else:
        _dsa_topk_bias(qi_ref, kwk_ref, kwq_ref, key_ref, hi_ref, lo_ref, bias_ref,
                       t0=t0, k_sel=k_sel, side_work=rwkv)
    for _ in rwkv:
        pass

    group = A_HEADS // A_KV_HEADS
    low = lax.broadcasted_iota(jnp.int32, (tq, LANES), 1) < A_HEAD_DIM
    for bb in range(nb):
        bias = bias_ref[bb * tq:(bb + 1) * tq, :]
        kk = k_ref[bb]
        for h2 in range(A_HEADS // 2):
            res = []
            for h in (2 * h2, 2 * h2 + 1):
                n = h // group
                kn = kk[:, n * A_HEAD_DIM:(n + 1) * A_HEAD_DIM]
                s = _dot_nt(q_ref[bb, :, h * A_HEAD_DIM:(h + 1) * A_HEAD_DIM], kn) + bias
                m = jnp.max(s, axis=1, keepdims=True)
                p = jnp.exp2(s - m).astype(BF16)
                vx = v_ref[bb, :, (2 * n + h % 2) * LANES:(2 * n + h % 2 + 1) * LANES]
                oe = _dot(p, vx)
                res.append(oe / pltpu.roll(oe, A_HEAD_DIM, 1))
            o_ref[bb, :, h2 * LANES:(h2 + 1) * LANES] = jnp.where(low, res[0],
                                                                   res[1]).astype(BF16)


def _dsa_rwkv(q, k, v, qi, kw, k_sel, rw_streams, rw_vecs):
    b, s, _ = q.shape
    tq = min(DSA_Q_TILE, s)
    w = rw_streams[0].shape[2]
    c = min(RW_CHUNK, s)
    state = jnp.zeros((b, w // LANES, LANES, LANES), F32)
    ya, yc = [], []
    for first in range(s // tq):
        sk = (first + 1) * tq
        qmap = lambda i, j, first=first: (i, first, 0)
        kmap = lambda i, j: (i, 0, 0)
        st_spec = pl.BlockSpec((1, w // LANES, LANES, LANES), lambda i, j: (i, 0, 0, 0))
        tile = lambda width: pl.BlockSpec((1, tq, width), lambda i, j: (i, 0, 0))
        o, y, state = pl.pallas_call(
            functools.partial(_dsa_rwkv_kernel, first_block=first, k_sel=k_sel, rw_chunk=c),
            grid=(b, 1),
            in_specs=[pl.BlockSpec((1, tq, A_WIDTH), qmap), pl.BlockSpec((1, sk, LANES), kmap),
                      pl.BlockSpec((1, sk, 4 * LANES), kmap),
                      pl.BlockSpec((1, tq, IDX_WIDTH), qmap),
                      pl.BlockSpec((1, sk, LANES), kmap), pl.BlockSpec((1, tq, LANES), qmap)]
                     + [pl.BlockSpec((1, tq, w), qmap)] * N_RW_STREAMS
                     + [pl.BlockSpec((1, w), lambda i, j: (0, 0))] * 3 + [st_spec],
            out_specs=[tile(A_WIDTH), tile(w), st_spec],
            out_shape=[jax.ShapeDtypeStruct((b, tq, A_WIDTH), BF16),
                       jax.ShapeDtypeStruct((b, tq, w), BF16),
                       jax.ShapeDtypeStruct(state.shape, F32)],
            scratch_shapes=[pltpu.VMEM((tq, sk), jnp.int32), pltpu.VMEM((tq, sk), jnp.int16),
                            pltpu.VMEM((tq, sk), jnp.int16), pltpu.VMEM((tq, sk), F32)],
            compiler_params=_cparams(("arbitrary", "arbitrary")),
            name=f"dsa_rwkv_{sk}",
        )(q, k, v, qi, kw, kw, *rw_streams, *rw_vecs, state)
        ya.append(o)
        yc.append(y)
    return jnp.concatenate(ya, axis=1), jnp.concatenate(yc, axis=1)


def _rwkv_stages(r_ref, lw_ref, k_ref, v_ref, kn_ref, a_ref, g_ref, rk_ref, lnw_ref, lnb_ref,
                 st_in_ref, y_ref, st_out_ref, *, c):
    width = r_ref.shape[2]
    nc = r_ref.shape[1] // c
    hd = RW_HEAD_DIM

    lw_all = lw_ref[0]
    rows = lax.broadcasted_iota(jnp.int32, lw_all.shape, 0) & (c - 1)
    cum_all = lw_all
    step = 1
    while step < c:
        cum_all = cum_all + jnp.where(rows >= step, pltpu.roll(cum_all, step, 0), 0.0)
        step *= 2

    lane = lax.broadcasted_iota(jnp.int32, (c, LANES), 1)
    first = lane < hd
    r2 = lax.broadcasted_iota(jnp.int32, (2 * c, 2 * c), 0)
    c2 = lax.broadcasted_iota(jnp.int32, (2 * c, 2 * c), 1)
    same_head = (r2 >= c) == (c2 >= c)
    t_row = r2 & (c - 1)
    t_col = c2 & (c - 1)
    strict = same_head & (t_col < t_row)
    lower = same_head & (t_col <= t_row)

    def blk(x):
        return jnp.concatenate([jnp.where(first, x, 0.0), jnp.where(first, 0.0, x)],
                               axis=0).astype(BF16)

    def head_sum(x):
        s0 = jnp.sum(jnp.where(first, x, 0.0), axis=1, keepdims=True)
        s1 = jnp.sum(jnp.where(first, 0.0, x), axis=1, keepdims=True)
        return jnp.where(first, s0, s1)

    pairs = range(width // LANES)
    sls = [slice(p * LANES, (p + 1) * LANES) for p in pairs]
    units = [(j, p) for j in range(nc) for p in pairs]
    n = 2 * c
    la, lr, vb, bk_t, bk_h, p_last, vals, bonus_in = {}, {}, {}, {}, {}, {}, {}, {}
    for j in range(nc):
        rs = slice(j * c, (j + 1) * c)
        r, kf, v = (x[0, rs, :].astype(F32) for x in (r_ref, k_ref, v_ref))
        kn, a = kn_ref[0, rs, :].astype(F32), a_ref[0, rs, :].astype(F32)
        lw, cum = lw_all[rs], cum_all[rs]
        cum_last = cum[c - 1:c, :]
        p_inv = jnp.exp(-cum)
        p_rem = jnp.exp(cum_last - cum)
        kb = kn * a
        a_t = -kn * jnp.exp(cum - lw)
        r_t = r * jnp.exp(cum)
        b_t = kb * p_inv
        k_t = kf * p_inv
        b_h = kb * p_rem
        k_h = kf * p_rem
        p_last[j] = jnp.exp(cum_last)
        vals[j] = v
        bonus_in[j] = r * kf * rk_ref[...]
        for p in pairs:
            sl = sls[p]
            la[j, p], lr[j, p], vb[j, p] = blk(a_t[:, sl]), blk(r_t[:, sl]), blk(v[:, sl])
            bk_t[j, p] = jnp.concatenate([blk(b_t[:, sl]), blk(k_t[:, sl])], axis=0)
            bk_h[j, p] = jnp.concatenate([blk(b_h[:, sl]), blk(k_h[:, sl])], axis=0)
        yield
    gram = {u: _dot_nt(jnp.concatenate([la[u], lr[u]], axis=0), bk_t[u]) for u in units}
    x = {u: jnp.where(strict, gram[u][:n, :n], 0.0) for u in units}
    npow = dict(x)
    yield
    for _ in range(int(math.log2(c)) - 1):
        npow_b = {u: npow[u].astype(BF16) for u in units}
        npow = {u: _dot(npow_b[u], npow_b[u]) for u in units}
        x = {u: x[u] + npow[u] + _dot(x[u].astype(BF16), npow[u].astype(BF16)) for u in units}
        yield
    x_b = {u: x[u].astype(BF16) for u in units}
    a_ak = {u: jnp.where(strict, gram[u][:n, n:], 0.0).astype(BF16) for u in units}
    lower2 = jnp.concatenate([lower, lower], axis=1)
    a_rbk = {u: jnp.where(lower2, gram[u][n:, :], 0.0).astype(BF16) for u in units}
    akv = {u: _dot(a_ak[u], vb[u]) for u in units}
    yield

    st = [st_in_ref[0, p] for p in pairs]
    for j in range(nc):
        st_b = [s.astype(BF16) for s in st]
        rhs = [_dot_nt(la[j, p], st_b[p]) + akv[j, p] for p in pairs]
        sa = [rhs[p] + _dot(x_b[j, p], rhs[p].astype(BF16)) for p in pairs]
        yield
        sav = [jnp.concatenate([sa[p].astype(BF16), vb[j, p]], axis=0) for p in pairs]
        yb = [_dot_nt(lr[j, p], st_b[p]) + _dot(a_rbk[j, p], sav[p]) for p in pairs]
        st = [st[p] * p_last[j][:, sls[p]] + _dot_tn(sav[p], bk_h[j, p]) for p in pairs]
        yield
        rs = slice(j * c, (j + 1) * c)
        for p in pairs:
            sl = sls[p]
            y = yb[p][:c] + yb[p][c:]
            mean = head_sum(y) * (1.0 / hd)
            yc = y - mean
            var = head_sum(yc * yc) * (1.0 / hd)
            yn = yc * lax.rsqrt(var + RW_GN_EPS) * lnw_ref[:, sl] + lnb_ref[:, sl]
            bonus = head_sum(bonus_in[j][:, sl]) * vals[j][:, sl]
            y_ref[0, rs, sl] = ((yn + bonus) * g_ref[0, rs, sl]).astype(BF16)
        yield
    for p in pairs:
        st_out_ref[0, p] = st[p]


def _merge_ffn_kernel(h_ref, ya_ref, yb_ref, yc_ref, gt_ref, wb_ref, wo_ref,
                      g_ref, wup_ref, cw_ref, wd_ref, gf_ref, o_ref, act_ref, carry_ref, *,
                      final_norm):
    @pl.when(pl.program_id(1) == 0)
    def _():
        carry_ref[...] = jnp.zeros_like(carry_ref)

    ts, d = h_ref.shape[1], h_ref.shape[2]
    dff = wd_ref.shape[0]
    mixed = None
    for i, y_ref in enumerate((ya_ref, yb_ref, yc_ref)):
        up = _dot(y_ref[0], wb_ref[i])
        term = gt_ref[0, :, i * d:(i + 1) * d].astype(F32) * up
        mixed = term if mixed is None else mixed + term
    h_mid = h_ref[0] + _dot(mixed.astype(BF16), wo_ref[...])
    xn = _rms(h_mid, g_ref[...])

    def conv(sl):
        u = _dot(xn, wup_ref[:, sl])
        prev = carry_ref[:, sl]
        out = (_shift_rows(u, prev, 2) * cw_ref[0:1, sl] + _shift_rows(u, prev, 1) * cw_ref[1:2, sl]
               + u * cw_ref[2:3, sl])
        carry_ref[:, sl] = u[ts - SUBLANES:ts]
        return out

    for lo in range(0, dff, MXU_COLS):
        gate = conv(slice(lo, lo + MXU_COLS))
        up = conv(slice(dff + lo, dff + lo + MXU_COLS))
        act_ref[:, lo:lo + MXU_COLS] = (gate * jax.nn.sigmoid(gate) * up).astype(BF16)
    out = h_mid + _dot(act_ref[...], wd_ref[...])
    if final_norm:
        ms = jnp.mean(out * out, axis=-1, keepdims=True)
        out = out * lax.rsqrt(ms + RMS_EPS) * gf_ref[...]
    o_ref[0] = out


def _merge_ffn(h, ya, yb, yc, gates, wb, wo, g, w_up, w_conv, w_down, g_final, final_norm, ts):
    b, s, d = h.shape
    dff = w_down.shape[0]
    bw = ya.shape[2]
    row = lambda i, j: (i, j, 0)
    const = lambda i, j: (0, 0)
    resident = dict(pipeline_mode=pl.Buffered(1))
    return pl.pallas_call(
        functools.partial(_merge_ffn_kernel, final_norm=final_norm),
        grid=(b, s // ts),
        in_specs=[pl.BlockSpec((1, ts, d), row)] + [pl.BlockSpec((1, ts, bw), row)] * 3
                 + [pl.BlockSpec((1, ts, N_BRANCH * d), row),
                    pl.BlockSpec((N_BRANCH, bw, d), lambda i, j: (0, 0, 0), **resident),
                    pl.BlockSpec((d, d), const, **resident),
                    pl.BlockSpec((1, d), const),
                    pl.BlockSpec((d, 2 * dff), const, **resident),
                    pl.BlockSpec((3, 2 * dff), const),
                    pl.BlockSpec((dff, d), const, **resident),
                    pl.BlockSpec((1, d), const)],
        out_specs=pl.BlockSpec((1, ts, d), row),
        out_shape=jax.ShapeDtypeStruct((b, s, d), F32),
        scratch_shapes=[pltpu.VMEM((ts, dff), BF16), pltpu.VMEM((SUBLANES, 2 * dff), F32)],
        compiler_params=pltpu.CompilerParams(dimension_semantics=("arbitrary", "arbitrary"),
                                             vmem_limit_bytes=WIDE_VMEM_LIMIT),
        name="merge_ffn",
    )(h, ya, yb, yc, gates, wb, wo, g, w_up, w_conv, w_down, g_final)


def _rope_tables(positions, head_dim):
    rot = head_dim // ROPE_FRACTION
    half = rot // 2
    inv = ROPE_THETA ** (-jnp.arange(half, dtype=F32) * 2.0 / rot)
    ang = positions.astype(F32)[:, :, None] * inv
    cos, sin = jnp.cos(ang), jnp.sin(ang)
    zeros = jnp.zeros_like(cos)
    rest = head_dim - rot
    pad1 = jnp.ones(cos.shape[:2] + (rest,), F32)
    pad0 = jnp.zeros(cos.shape[:2] + (rest,), F32)
    reps = LANES // head_dim
    c = jnp.tile(jnp.concatenate([cos, cos, pad1], axis=-1), (1, 1, reps))
    s1 = jnp.tile(jnp.concatenate([-sin, zeros, pad0], axis=-1), (1, 1, reps))
    s2 = jnp.tile(jnp.concatenate([zeros, sin, pad0], axis=-1), (1, 1, reps))
    return c, s1, s2


def kernel(x, positions, norm_mix, w_in, b_gate, sc_conv, rw_mu, rw_w0, rw_w_up, rw_a0, rw_a_up, rw_g_up, rw_k_k, rw_k_a, rw_r_k, rw_ln_w, rw_ln_b, w_branch, w_out, norm_ffn, ffn_up, ffn_conv, ffn_down, norm_final):
    bsz, seq, d = x.shape
    depth = w_in.shape[0]
    ts = min(ROW_TILE, seq)
    k_sel = min(TOPK_MAX, seq // 4)

    tabs = _rope_tables(positions, A_HEAD_DIM) + _rope_tables(positions, IDX_DIM)
    o_q = 0
    o_k = o_q + A_WIDTH
    o_v = o_k + A_KV_WIDTH
    o_qi = o_v + A_KV_WIDTH
    o_ki = o_qi + IDX_WIDTH
    o_wi = o_ki + IDX_DIM
    o_sc = o_wi + IDX_HEADS
    o_rw = o_sc + 3 * SC_WIDTH
    o_gate = o_rw + RW_IN
    attn_scale = A_HEAD_DIM ** -0.5 * math.log2(math.e)
    idx_scale = (IDX_HEADS ** -0.5) * (IDX_DIM ** -0.5)
    head_sum = jnp.kron(jnp.eye(MXU_COLS // RW_HEAD_DIM, dtype=F32),
                        jnp.ones((RW_HEAD_DIM, RW_HEAD_DIM), F32)).astype(BF16)

    h = x
    for l in range(depth):
        w = w_in[l]
        pad = jnp.zeros((d, LANES - IDX_DIM - IDX_HEADS), F32)
        w_attn = jnp.concatenate(
            [w[:, o_q:o_k] * attn_scale, w[:, o_k:o_ki], w[:, o_ki:o_wi],
             w[:, o_wi:o_sc] * idx_scale, pad], axis=1).astype(BF16)
        w_sc = w[:, o_sc:o_rw].astype(BF16)
        w_rw = w[:, o_rw:o_gate].astype(BF16)
        w_gate = w[:, o_gate:].astype(BF16)
        g_mix = norm_mix[l][None, :]

        zero_lora = jnp.zeros((RW_W_LORA, RW_WIDTH), F32)
        w_lora = jnp.concatenate(
            [jnp.concatenate([rw_w_up[l], zero_lora], axis=1),
             jnp.concatenate([zero_lora, rw_a_up[l]], axis=1)], axis=0).astype(BF16)
        (q, k, v, qi, kw), y_b, (r, lw, kf, vv, kn, a, gg), gates = _in_proj(
            h, g_mix,
            (w_attn,) + tabs,
            (w_sc, sc_conv[l]),
            (w_rw, rw_mu[l][None, :], w_lora, rw_g_up[l].astype(BF16), rw_w0[l][None, :],
             rw_a0[l][None, :], rw_k_k[l][None, :], rw_k_a[l][None, :], head_sum),
            (w_gate, b_gate[l][None, :]), ts)

        y_a, y_c = _dsa_rwkv(
            q, k, v, qi, kw, k_sel, (r, lw, kf, vv, kn, a, gg),
            (rw_r_k[l].reshape(1, RW_WIDTH), rw_ln_w[l][None, :], rw_ln_b[l][None, :]))

        h = _merge_ffn(h, y_a, y_b, y_c, gates, w_branch[l].astype(BF16), w_out[l].astype(BF16),
                       norm_ffn[l][None, :], ffn_up[l].astype(BF16), ffn_conv[l],
                       ffn_down[l].astype(BF16), norm_final[None, :], l == depth - 1, ts)
    return h
```

```python
import functools
import math

import jax
import jax.numpy as jnp
from jax import lax
from jax.experimental import pallas as pl
from jax.experimental.pallas import tpu as pltpu

F32 = jnp.float32
BF16 = jnp.bfloat16

A_HEADS = 8
A_KV_HEADS = 2
A_HEAD_DIM = 64
A_WIDTH = A_HEADS * A_HEAD_DIM
A_KV_WIDTH = A_KV_HEADS * A_HEAD_DIM
IDX_HEADS = 8
IDX_DIM = 32
IDX_WIDTH = IDX_HEADS * IDX_DIM
TOPK_MAX = 256
ROPE_THETA = 500000.0
ROPE_FRACTION = 4
SC_WIDTH = 512
RW_HEADS = 8
RW_HEAD_DIM = 64
RW_WIDTH = RW_HEADS * RW_HEAD_DIM
RW_W_LORA = 64
RW_A_LORA = 64
RW_G_LORA = 128
RW_IN = 3 * RW_WIDTH + RW_W_LORA + RW_A_LORA + RW_G_LORA
RW_GN_EPS = 64e-5
RW_DECAY_SCALE = math.exp(-0.5)
N_BRANCH = 3
RMS_EPS = 1e-6

LANES = 128
MXU_COLS = 256
SUBLANES = 8
INT_MIN = -2147483648
VMEM_LIMIT = 48 * 1024 * 1024
WIDE_VMEM_LIMIT = 56 * 1024 * 1024

ROW_TILE = 512
DSA_Q_TILE = 256
RW_CHUNK = 64


def _cparams(sem):
    return pltpu.CompilerParams(dimension_semantics=sem, vmem_limit_bytes=VMEM_LIMIT)


def _rms(x, g):
    ms = jnp.mean(x * x, axis=-1, keepdims=True)
    return (x * lax.rsqrt(ms + RMS_EPS) * g).astype(BF16)


def _dot(a, b):
    return jnp.dot(a, b, preferred_element_type=F32)


def _dot_nt(a, b):
    return lax.dot_general(a, b, (((1,), (1,)), ((), ())), preferred_element_type=F32)


def _dot_tn(a, b):
    return lax.dot_general(a, b, (((0,), (0,)), ((), ())), preferred_element_type=F32)


def _shift_rows(x, prev, k):
    xr = pltpu.roll(x, k, 0)
    pr = pltpu.roll(prev, k, 0)
    rows = lax.broadcasted_iota(jnp.int32, pr.shape, 0)
    head = jnp.where(rows < k, pr, xr[0:SUBLANES])
    return jnp.concatenate([head, xr[SUBLANES:]], axis=0)


def _rope_slab(y, c, s1, s2, half):
    return y * c + pltpu.roll(y, LANES - half, 1) * s1 + pltpu.roll(y, half, 1) * s2


def _attn_proj_body(xn, w_ref, cq_ref, s1q_ref, s2q_ref, ci_ref, s1i_ref, s2i_ref,
                    q_ref, k_ref, v_ref, qi_ref, kw_ref):
    cq, s1q, s2q = cq_ref[0], s1q_ref[0], s2q_ref[0]
    ci, s1i, s2i = ci_ref[0], s1i_ref[0], s2i_ref[0]
    hq = A_HEAD_DIM // ROPE_FRACTION // 2
    hi = IDX_DIM // ROPE_FRACTION // 2
    col = 0
    for p in range(A_WIDTH // MXU_COLS):
        y = _dot(xn, w_ref[:, col:col + MXU_COLS])
        for half in range(MXU_COLS // LANES):
            lo = p * MXU_COLS + half * LANES
            q_ref[0, :, lo:lo + LANES] = _rope_slab(
                y[:, half * LANES:(half + 1) * LANES], cq, s1q, s2q, hq).astype(BF16)
        col += MXU_COLS
    y = _dot(xn, w_ref[:, col:col + 2 * LANES])
    k_ref[0] = _rope_slab(y[:, :LANES], cq, s1q, s2q, hq).astype(BF16)
    val = y[:, LANES:]
    swapped = pltpu.roll(val, A_HEAD_DIM, 1)
    low = lax.broadcasted_iota(jnp.int32, val.shape, 1) < A_HEAD_DIM
    slabs = (jnp.where(low, val, 1.0), jnp.where(low, 1.0, swapped),
             jnp.where(low, swapped, 1.0), jnp.where(low, 1.0, val))
    for i, slab in enumerate(slabs):
        v_ref[0, :, i * LANES:(i + 1) * LANES] = slab.astype(BF16)
    col += 2 * LANES
    y = _dot(xn, w_ref[:, col:col + IDX_WIDTH])
    for half in range(IDX_WIDTH // LANES):
        qi_ref[0, :, half * LANES:(half + 1) * LANES] = _rope_slab(
            y[:, half * LANES:(half + 1) * LANES], ci, s1i, s2i, hi).astype(BF16)
    col += IDX_WIDTH
    y = _dot(xn, w_ref[:, col:col + LANES])
    lane = lax.broadcasted_iota(jnp.int32, y.shape, 1)
    is_key = lane < IDX_DIM
    ck = jnp.where(is_key, ci, 1.0)
    kw_ref[0] = _rope_slab(y, ck, jnp.where(is_key, s1i, 0.0), jnp.where(is_key, s2i, 0.0), hi)


def _sc_proj_body(xn, w_ref, cw_ref, y_ref, carry_ref):
    ts = xn.shape[0]
    wd = MXU_COLS
    for lo in range(0, SC_WIDTH, wd):
        u = _dot(xn, w_ref[:, lo:lo + wd])
        gate_b = _dot(xn, w_ref[:, SC_WIDTH + lo:SC_WIDTH + lo + wd])
        gate_c = _dot(xn, w_ref[:, 2 * SC_WIDTH + lo:2 * SC_WIDTH + lo + wd])
        cu = gate_c * u
        prev = carry_ref[:, lo:lo + wd]
        conv = (_shift_rows(cu, prev, 2) * cw_ref[0:1, lo:lo + wd]
                + _shift_rows(cu, prev, 1) * cw_ref[1:2, lo:lo + wd]
                + cu * cw_ref[2:3, lo:lo + wd])
        y_ref[0, :, lo:lo + wd] = (gate_b * conv).astype(BF16)
        carry_ref[:, lo:lo + wd] = cu[ts - SUBLANES:ts]


def _split_dot(x, m):
    hi = x.astype(BF16)
    lo = (x - hi.astype(F32)).astype(BF16)
    return _dot(hi, m) + _dot(lo, m)


def _rw_proj_body(xn, w_ref, mu_ref, wlora_ref, gup_ref, w0_ref, a0_ref, kk_ref, ka_ref, hsum_ref,
                  r_ref, lw_ref, k_ref, v_ref, kn_ref, a_ref, gg_ref, carry_ref):
    ts = xn.shape[0]

    def mixed(lo, width):
        z = _dot(xn, w_ref[:, lo:lo + width])
        zp = _shift_rows(z, carry_ref[:, lo:lo + width], 1)
        carry_ref[:, lo:lo + width] = z[ts - SUBLANES:ts]
        return z + (zp - z) * mu_ref[:, lo:lo + width]

    lora_lo = 3 * RW_WIDTH
    zlg = mixed(lora_lo, RW_IN - lora_lo)
    zl = zlg[:, :LANES]
    lane = lax.broadcasted_iota(jnp.int32, zl.shape, 1)
    zl = jnp.where(lane < RW_W_LORA, jnp.tanh(zl), zl).astype(BF16)
    zg = jax.nn.sigmoid(zlg[:, LANES:]).astype(BF16)
    wd = MXU_COLS
    for lo in range(0, RW_WIDTH, wd):
        sl = slice(lo, lo + wd)
        up = _dot(zl, wlora_ref[:, lo:lo + wd])
        ua = _dot(zl, wlora_ref[:, RW_WIDTH + lo:RW_WIDTH + lo + wd])
        lw_ref[0, :, sl] = -RW_DECAY_SCALE * jax.nn.sigmoid(w0_ref[:, sl] + up)
        a = jax.nn.sigmoid(a0_ref[:, sl] + ua)
        a_ref[0, :, sl] = a.astype(BF16)
        gg_ref[0, :, sl] = _dot(zg, gup_ref[:, sl]).astype(BF16)
        r_ref[0, :, sl] = mixed(lo, wd).astype(BF16)
        k = mixed(RW_WIDTH + lo, wd)
        v_ref[0, :, sl] = mixed(2 * RW_WIDTH + lo, wd).astype(BF16)
        kk = k * kk_ref[:, sl]
        ss = _split_dot(kk * kk, hsum_ref[...])
        kn_ref[0, :, sl] = (kk * lax.rsqrt(jnp.maximum(ss, 1e-24))).astype(BF16)
        k_ref[0, :, sl] = (k * (1.0 + (a - 1.0) * ka_ref[:, sl])).astype(BF16)


def _gate_proj_body(xn, w_ref, b_ref, o_ref):
    n = w_ref.shape[1]
    step = 4 * LANES
    for lo in range(0, n, step):
        y = _dot(xn, w_ref[:, lo:lo + step]) + b_ref[:, lo:lo + step]
        o_ref[0, :, lo:lo + step] = jax.nn.sigmoid(y).astype(BF16)


N_ATTN_IN, N_SC_IN, N_RW_IN, N_GATE_IN = 7, 2, 9, 2
N_ATTN_OUT, N_SC_OUT, N_RW_OUT = 5, 1, 7


def _in_proj_kernel(*refs):
    refs = list(refs)

    def take(n):
        taken, refs[:] = refs[:n], refs[n:]
        return taken

    h_ref, g_ref = take(2)
    attn_in, sc_in, rw_in, gate_in = (take(n) for n in (N_ATTN_IN, N_SC_IN, N_RW_IN, N_GATE_IN))
    attn_out, sc_out, rw_out = (take(n) for n in (N_ATTN_OUT, N_SC_OUT, N_RW_OUT))
    gate_out, sc_carry, rw_carry = take(3)

    @pl.when(pl.program_id(1) == 0)
    def _():
        sc_carry[...] = jnp.zeros_like(sc_carry)
        rw_carry[...] = jnp.zeros_like(rw_carry)

    xn = _rms(h_ref[0], g_ref[...])
    _attn_proj_body(xn, *attn_in, *attn_out)
    _sc_proj_body(xn, *sc_in, *sc_out, sc_carry)
    _rw_proj_body(xn, *rw_in, *rw_out, rw_carry)
    _gate_proj_body(xn, *gate_in, gate_out)


def _in_proj(h, g, attn_in, sc_in, rw_in, gate_in, ts):
    b, s, d = h.shape
    row = lambda i, j: (i, j, 0)
    const = lambda i, j: (0, 0)

    def seq_spec(width):
        return pl.BlockSpec((1, ts, width), row)

    def whole(x):
        if x.size * x.dtype.itemsize >= (1 << 20):
            return pl.BlockSpec(x.shape, const, pipeline_mode=pl.Buffered(1))
        return pl.BlockSpec(x.shape, const)

    w_attn, tabs = attn_in[0], attn_in[1:]
    in_specs = ([seq_spec(d), whole(g), whole(w_attn)] + [seq_spec(LANES)] * len(tabs)
                + [whole(x) for x in sc_in] + [whole(x) for x in rw_in]
                + [whole(x) for x in gate_in])
    out_widths = ([(A_WIDTH, BF16), (LANES, BF16), (4 * LANES, BF16), (IDX_WIDTH, BF16),
                   (LANES, F32), (SC_WIDTH, BF16)]
                  + [(RW_WIDTH, F32 if i == 1 else BF16) for i in range(N_RW_OUT)]
                  + [(gate_in[0].shape[1], BF16)])
    outs = pl.pallas_call(
        _in_proj_kernel,
        grid=(b, s // ts),
        in_specs=in_specs,
        out_specs=[seq_spec(w) for w, _ in out_widths],
        out_shape=[jax.ShapeDtypeStruct((b, s, w), dt) for w, dt in out_widths],
        scratch_shapes=[pltpu.VMEM((SUBLANES, SC_WIDTH), F32), pltpu.VMEM((SUBLANES, RW_IN), F32)],
        compiler_params=pltpu.CompilerParams(dimension_semantics=("arbitrary", "arbitrary"),
                                             vmem_limit_bytes=WIDE_VMEM_LIMIT),
        name="in_proj",
    )(h, g, *attn_in, *sc_in, *rw_in, *gate_in)
    return outs[:5], outs[5], outs[6:13], outs[13]


def _dsa_topk_bias(qi_ref, kwk_ref, kwq_ref, key_ref, hi_ref, lo_ref, bias_ref, *, t0, k_sel,
                   side_work):
    tq, sk = bias_ref.shape
    half_min = -(1 << 15)

    ki = kwk_ref[0][:, 0:IDX_DIM].astype(BF16)
    wq = kwq_ref[0]
    qi = qi_ref[0]
    isc = jnp.zeros((tq, sk), F32)
    for h in range(IDX_HEADS):
        rel = jnp.maximum(_dot_nt(qi[:, h * IDX_DIM:(h + 1) * IDX_DIM], ki), 0.0)
        isc = isc + wq[:, IDX_DIM + h:IDX_DIM + h + 1] * rel

    cols = lax.broadcasted_iota(jnp.int32, (tq, sk), 1)
    rows = t0 + lax.broadcasted_iota(jnp.int32, (tq, sk), 0)
    bits = pltpu.bitcast(isc, jnp.int32)
    key = bits ^ ((bits >> 31) & jnp.int32(0x7FFFFFFF))
    key = jnp.where(cols <= rows, key, INT_MIN)
    key_ref[...] = key
    hi_ref[...] = (key >> 16).astype(jnp.int16)
    lo_ref[...] = ((key & jnp.int32(0xFFFF)) + half_min).astype(jnp.int16)


    def search_half(src_ref, cnt0):
        tau_u, cnt_tau = jnp.zeros((tq, 1), jnp.int32), cnt0
        for it in range(16):
            cand_u = tau_u | (1 << (15 - it))
            cand = jnp.broadcast_to(cand_u + half_min, (tq, LANES)).astype(jnp.int16)
            accs = [jnp.zeros((tq, LANES), jnp.int16) for _ in range(4)]
            for j in range(sk // LANES):
                hit = src_ref[:, j * LANES:(j + 1) * LANES] >= cand
                accs[j % 4] = accs[j % 4] + jnp.where(hit, jnp.int16(1), jnp.int16(0))
            acc = (accs[0] + accs[1]) + (accs[2] + accs[3])
            cnt = jnp.sum(acc.astype(F32), axis=1, keepdims=True)
            ok = cnt >= k_sel
            tau_u, cnt_tau = jnp.where(ok, cand_u, tau_u), jnp.where(ok, cnt, cnt_tau)
            next(side_work, None)
        return tau_u, cnt_tau

    tau_hi_u, cnt_hi = search_half(hi_ref, jnp.full((tq, 1), float(sk), F32))
    tau_hi = tau_hi_u + half_min
    t_hi = jnp.broadcast_to(tau_hi, (tq, LANES)).astype(jnp.int16)
    for j in range(sk // LANES):
        sl = slice(j * LANES, (j + 1) * LANES)
        hi_j = hi_ref[:, sl]
        inside = jnp.where(hi_j == t_hi, lo_ref[:, sl], jnp.int16(half_min))
        lo_ref[:, sl] = jnp.where(hi_j > t_hi, jnp.int16(-half_min - 1), inside)
    tau_lo_u, cnt_ge = search_half(lo_ref, cnt_hi)
    tau = (tau_hi << 16) | tau_lo_u

    drop = jnp.where(tau == INT_MIN, float(2 * sk), cnt_ge - k_sel)
    r2 = lax.broadcasted_iota(jnp.int32, (LANES, 2 * LANES), 0)
    c2 = lax.broadcasted_iota(jnp.int32, (LANES, 2 * LANES), 1)
    tri = jnp.where((r2 > c2) | (c2 >= LANES), 1.0, 0.0).astype(BF16)
    after = jnp.zeros((tq, LANES), F32)
    for j in reversed(range(sk // LANES)):
        sl = slice(j * LANES, (j + 1) * LANES)
        key_j = key_ref[:, sl]
        eq = key_j == tau
        pr = _dot(jnp.where(eq, 1.0, 0.0).astype(BF16), tri)
        tied = jnp.where(eq, jnp.where(after + pr[:, :LANES] >= drop, 0.0, -jnp.inf), -jnp.inf)
        bias_ref[:, sl] = jnp.where(key_j > tau, 0.0, tied)
        after = after + pr[:, LANES:]


N_RW_STREAMS = 7
N_RW_REFS = N_RW_STREAMS + 4


def _dsa_rwkv_kernel(*refs, first_block, k_sel, rw_chunk):
    q_ref, k_ref, v_ref, qi_ref, kwk_ref, kwq_ref = refs[:6]
    rw_in = refs[6:6 + N_RW_REFS]
    o_ref, y_ref, st_out_ref, key_ref, hi_ref, lo_ref, bias_ref = refs[6 + N_RW_REFS:]
    rwkv = _rwkv_stages(*rw_in, y_ref, st_out_ref, c=rw_chunk)
    tq = q_ref.shape[1]
    sk = k_ref.shape[1]
    t0 = first_block * tq
    if sk <= k_sel:
        cols = lax.broadcasted_iota(jnp.int32, (tq, sk), 1)
        rows = t0 + lax.broadcasted_iota(jnp.int32, (tq, sk), 0)
        bias_ref[...] = jnp.where(cols <= rows, 0.0, -jnp.inf)
    else:
        _dsa_topk_bias(qi_ref, kwk_ref, kwq_ref, key_ref, hi_ref, lo_ref, bias_ref,
                       t0=t0, k_sel=k_sel, side_work=rwkv)
    for _ in rwkv:
        pass

    group = A_HEADS // A_KV_HEADS
    low = lax.broadcasted_iota(jnp.int32, (tq, LANES), 1) < A_HEAD_DIM
    bias = bias_ref[...]
    kk = k_ref[0]
    for h2 in range(A_HEADS // 2):
        res = []
        for h in (2 * h2, 2 * h2 + 1):
            n = h // group
            kn = kk[:, n * A_HEAD_DIM:(n + 1) * A_HEAD_DIM]
            s = _dot_nt(q_ref[0, :, h * A_HEAD_DIM:(h + 1) * A_HEAD_DIM], kn) + bias
            m = jnp.max(s, axis=1, keepdims=True)
            p = jnp.exp2(s - m).astype(BF16)
            vx = v_ref[0, :, (2 * n + h % 2) * LANES:(2 * n + h % 2 + 1) * LANES]
            oe = _dot(p, vx)
            res.append(oe / pltpu.roll(oe, A_HEAD_DIM, 1))
        o_ref[0, :, h2 * LANES:(h2 + 1) * LANES] = jnp.where(low, res[0], res[1]).astype(BF16)


def _dsa_rwkv(q, k, v, qi, kw, k_sel, rw_streams, rw_vecs):
    b, s, _ = q.shape
    tq = min(DSA_Q_TILE, s)
    w = rw_streams[0].shape[2]
    c = min(RW_CHUNK, s)
    state = jnp.zeros((b, w // LANES, LANES, LANES), F32)
    ya, yc = [], []
    for first in range(s // tq):
        sk = (first + 1) * tq
        qmap = lambda i, j, first=first: (i, first, 0)
        kmap = lambda i, j: (i, 0, 0)
        st_spec = pl.BlockSpec((1, w // LANES, LANES, LANES), lambda i, j: (i, 0, 0, 0))
        tile = lambda width: pl.BlockSpec((1, tq, width), lambda i, j: (i, 0, 0))
        o, y, state = pl.pallas_call(
            functools.partial(_dsa_rwkv_kernel, first_block=first, k_sel=k_sel, rw_chunk=c),
            grid=(b, 1),
            in_specs=[pl.BlockSpec((1, tq, A_WIDTH), qmap), pl.BlockSpec((1, sk, LANES), kmap),
                      pl.BlockSpec((1, sk, 4 * LANES), kmap),
                      pl.BlockSpec((1, tq, IDX_WIDTH), qmap),
                      pl.BlockSpec((1, sk, LANES), kmap), pl.BlockSpec((1, tq, LANES), qmap)]
                     + [pl.BlockSpec((1, tq, w), qmap)] * N_RW_STREAMS
                     + [pl.BlockSpec((1, w), lambda i, j: (0, 0))] * 3 + [st_spec],
            out_specs=[tile(A_WIDTH), tile(w), st_spec],
            out_shape=[jax.ShapeDtypeStruct((b, tq, A_WIDTH), BF16),
                       jax.ShapeDtypeStruct((b, tq, w), BF16),
                       jax.ShapeDtypeStruct(state.shape, F32)],
            scratch_shapes=[pltpu.VMEM((tq, sk), jnp.int32), pltpu.VMEM((tq, sk), jnp.int16),
                            pltpu.VMEM((tq, sk), jnp.int16), pltpu.VMEM((tq, sk), F32)],
            compiler_params=_cparams(("arbitrary", "arbitrary")),
            name=f"dsa_rwkv_{sk}",
        )(q, k, v, qi, kw, kw, *rw_streams, *rw_vecs, state)
        ya.append(o)
        yc.append(y)
    return jnp.concatenate(ya, axis=1), jnp.concatenate(yc, axis=1)


def _rwkv_stages(r_ref, lw_ref, k_ref, v_ref, kn_ref, a_ref, g_ref, rk_ref, lnw_ref, lnb_ref,
                 st_in_ref, y_ref, st_out_ref, *, c):
    width = r_ref.shape[2]
    nc = r_ref.shape[1] // c
    hd = RW_HEAD_DIM

    lw_all = lw_ref[0]
    rows = lax.broadcasted_iota(jnp.int32, lw_all.shape, 0) & (c - 1)
    cum_all = lw_all
    step = 1
    while step < c:
        cum_all = cum_all + jnp.where(rows >= step, pltpu.roll(cum_all, step, 0), 0.0)
        step *= 2

    lane = lax.broadcasted_iota(jnp.int32, (c, LANES), 1)
    first = lane < hd
    r2 = lax.broadcasted_iota(jnp.int32, (2 * c, 2 * c), 0)
    c2 = lax.broadcasted_iota(jnp.int32, (2 * c, 2 * c), 1)
    same_head = (r2 >= c) == (c2 >= c)
    t_row = r2 & (c - 1)
    t_col = c2 & (c - 1)
    strict = same_head & (t_col < t_row)
    lower = same_head & (t_col <= t_row)

    def blk(x):
        return jnp.concatenate([jnp.where(first, x, 0.0), jnp.where(first, 0.0, x)],
                               axis=0).astype(BF16)

    def head_sum(x):
        s0 = jnp.sum(jnp.where(first, x, 0.0), axis=1, keepdims=True)
        s1 = jnp.sum(jnp.where(first, 0.0, x), axis=1, keepdims=True)
        return jnp.where(first, s0, s1)

    pairs = range(width // LANES)
    sls = [slice(p * LANES, (p + 1) * LANES) for p in pairs]
    units = [(j, p) for j in range(nc) for p in pairs]
    n = 2 * c
    la, lr, vb, bk_t, bk_h, p_last, vals, bonus_in = {}, {}, {}, {}, {}, {}, {}, {}
    for j in range(nc):
        rs = slice(j * c, (j + 1) * c)
        r, kf, v = (x[0, rs, :].astype(F32) for x in (r_ref, k_ref, v_ref))
        kn, a = kn_ref[0, rs, :].astype(F32), a_ref[0, rs, :].astype(F32)
        lw, cum = lw_all[rs], cum_all[rs]
        cum_last = cum[c - 1:c, :]
        p_inv = jnp.exp(-cum)
        p_rem = jnp.exp(cum_last - cum)
        kb = kn * a
        a_t = -kn * jnp.exp(cum - lw)
        r_t = r * jnp.exp(cum)
        b_t = kb * p_inv
        k_t = kf * p_inv
        b_h = kb * p_rem
        k_h = kf * p_rem
        p_last[j] = jnp.exp(cum_last)
        vals[j] = v
        bonus_in[j] = r * kf * rk_ref[...]
        for p in pairs:
            sl = sls[p]
            la[j, p], lr[j, p], vb[j, p] = blk(a_t[:, sl]), blk(r_t[:, sl]), blk(v[:, sl])
            bk_t[j, p] = jnp.concatenate([blk(b_t[:, sl]), blk(k_t[:, sl])], axis=0)
            bk_h[j, p] = jnp.concatenate([blk(b_h[:, sl]), blk(k_h[:, sl])], axis=0)
        yield
    gram = {u: _dot_nt(jnp.concatenate([la[u], lr[u]], axis=0), bk_t[u]) for u in units}
    x = {u: jnp.where(strict, gram[u][:n, :n], 0.0) for u in units}
    npow = dict(x)
    yield
    for _ in range(int(math.log2(c)) - 1):
        npow_b = {u: npow[u].astype(BF16) for u in units}
        npow = {u: _dot(npow_b[u], npow_b[u]) for u in units}
        x = {u: x[u] + npow[u] + _dot(x[u].astype(BF16), npow[u].astype(BF16)) for u in units}
        yield
    x_b = {u: x[u].astype(BF16) for u in units}
    a_ak = {u: jnp.where(strict, gram[u][:n, n:], 0.0).astype(BF16) for u in units}
    lower2 = jnp.concatenate([lower, lower], axis=1)
    a_rbk = {u: jnp.where(lower2, gram[u][n:, :], 0.0).astype(BF16) for u in units}
    akv = {u: _dot(a_ak[u], vb[u]) for u in units}
    yield

    st = [st_in_ref[0, p] for p in pairs]
    for j in range(nc):
        st_b = [s.astype(BF16) for s in st]
        rhs = [_dot_nt(la[j, p], st_b[p]) + akv[j, p] for p in pairs]
        sa = [rhs[p] + _dot(x_b[j, p], rhs[p].astype(BF16)) for p in pairs]
        yield
        sav = [jnp.concatenate([sa[p].astype(BF16), vb[j, p]], axis=0) for p in pairs]
        yb = [_dot_nt(lr[j, p], st_b[p]) + _dot(a_rbk[j, p], sav[p]) for p in pairs]
        st = [st[p] * p_last[j][:, sls[p]] + _dot_tn(sav[p], bk_h[j, p]) for p in pairs]
        yield
        rs = slice(j * c, (j + 1) * c)
        for p in pairs:
            sl = sls[p]
            y = yb[p][:c] + yb[p][c:]
            mean = head_sum(y) * (1.0 / hd)
            yc = y - mean
            var = head_sum(yc * yc) * (1.0 / hd)
            yn = yc * lax.rsqrt(var + RW_GN_EPS) * lnw_ref[:, sl] + lnb_ref[:, sl]
            bonus = head_sum(bonus_in[j][:, sl]) * vals[j][:, sl]
            y_ref[0, rs, sl] = ((yn + bonus) * g_ref[0, rs, sl]).astype(BF16)
        yield
    for p in pairs:
        st_out_ref[0, p] = st[p]


def _merge_ffn_kernel(h_ref, ya_ref, yb_ref, yc_ref, gt_ref, wb_ref, wo_ref,
                      g_ref, wup_ref, cw_ref, wd_ref, gf_ref, o_ref, act_ref, carry_ref, *,
                      final_norm):
    @pl.when(pl.program_id(1) == 0)
    def _():
        carry_ref[...] = jnp.zeros_like(carry_ref)

    ts, d = h_ref.shape[1], h_ref.shape[2]
    dff = wd_ref.shape[0]
    mixed = None
    for i, y_ref in enumerate((ya_ref, yb_ref, yc_ref)):
        up = _dot(y_ref[0], wb_ref[i])
        term = gt_ref[0, :, i * d:(i + 1) * d].astype(F32) * up
        mixed = term if mixed is None else mixed + term
    h_mid = h_ref[0] + _dot(mixed.astype(BF16), wo_ref[...])
    xn = _rms(h_mid, g_ref[...])

    def conv(sl):
        u = _dot(xn, wup_ref[:, sl])
        prev = carry_ref[:, sl]
        out = (_shift_rows(u, prev, 2) * cw_ref[0:1, sl] + _shift_rows(u, prev, 1) * cw_ref[1:2, sl]
               + u * cw_ref[2:3, sl])
        carry_ref[:, sl] = u[ts - SUBLANES:ts]
        return out

    for lo in range(0, dff, MXU_COLS):
        gate = conv(slice(lo, lo + MXU_COLS))
        up = conv(slice(dff + lo, dff + lo + MXU_COLS))
        act_ref[:, lo:lo + MXU_COLS] = (gate * jax.nn.sigmoid(gate) * up).astype(BF16)
    out = h_mid + _dot(act_ref[...], wd_ref[...])
    if final_norm:
        ms = jnp.mean(out * out, axis=-1, keepdims=True)
        out = out * lax.rsqrt(ms + RMS_EPS) * gf_ref[...]
    o_ref[0] = out


def _merge_ffn(h, ya, yb, yc, gates, wb, wo, g, w_up, w_conv, w_down, g_final, final_norm, ts):
    b, s, d = h.shape
    dff = w_down.shape[0]
    bw = ya.shape[2]
    row = lambda i, j: (i, j, 0)
    const = lambda i, j: (0, 0)
    resident = dict(pipeline_mode=pl.Buffered(1))
    return pl.pallas_call(
        functools.partial(_merge_ffn_kernel, final_norm=final_norm),
        grid=(b, s // ts),
        in_specs=[pl.BlockSpec((1, ts, d), row)] + [pl.BlockSpec((1, ts, bw), row)] * 3
                 + [pl.BlockSpec((1, ts, N_BRANCH * d), row),
                    pl.BlockSpec((N_BRANCH, bw, d), lambda i, j: (0, 0, 0), **resident),
                    pl.BlockSpec((d, d), const, **resident),
                    pl.BlockSpec((1, d), const),
                    pl.BlockSpec((d, 2 * dff), const, **resident),
                    pl.BlockSpec((3, 2 * dff), const),
                    pl.BlockSpec((dff, d), const, **resident),
                    pl.BlockSpec((1, d), const)],
        out_specs=pl.BlockSpec((1, ts, d), row),
        out_shape=jax.ShapeDtypeStruct((b, s, d), F32),
        scratch_shapes=[pltpu.VMEM((ts, dff), BF16), pltpu.VMEM((SUBLANES, 2 * dff), F32)],
        compiler_params=pltpu.CompilerParams(dimension_semantics=("arbitrary", "arbitrary"),
                                             vmem_limit_bytes=WIDE_VMEM_LIMIT),
        name="merge_ffn",
    )(h, ya, yb, yc, gates, wb, wo, g, w_up, w_conv, w_down, g_final)


def _rope_tables(positions, head_dim):
    rot = head_dim // ROPE_FRACTION
    half = rot // 2
    inv = ROPE_THETA ** (-jnp.arange(half, dtype=F32) * 2.0 / rot)
    ang = positions.astype(F32)[:, :, None] * inv
    cos, sin = jnp.cos(ang), jnp.sin(ang)
    zeros = jnp.zeros_like(cos)
    rest = head_dim - rot
    pad1 = jnp.ones(cos.shape[:2] + (rest,), F32)
    pad0 = jnp.zeros(cos.shape[:2] + (rest,), F32)
    reps = LANES // head_dim
    c = jnp.tile(jnp.concatenate([cos, cos, pad1], axis=-1), (1, 1, reps))
    s1 = jnp.tile(jnp.concatenate([-sin, zeros, pad0], axis=-1), (1, 1, reps))
    s2 = jnp.tile(jnp.concatenate([zeros, sin, pad0], axis=-1), (1, 1, reps))
    return c, s1, s2


def kernel(x, positions, norm_mix, w_in, b_gate, sc_conv, rw_mu, rw_w0, rw_w_up, rw_a0, rw_a_up, rw_g_up, rw_k_k, rw_k_a, rw_r_k, rw_ln_w, rw_ln_b, w_branch, w_out, norm_ffn, ffn_up, ffn_conv, ffn_down, norm_final):
    bsz, seq, d = x.shape
    depth = w_in.shape[0]
    ts = min(ROW_TILE, seq)
    k_sel = min(TOPK_MAX, seq // 4)

    tabs = _rope_tables(positions, A_HEAD_DIM) + _rope_tables(positions, IDX_DIM)
    o_q = 0
    o_k = o_q + A_WIDTH
    o_v = o_k + A_KV_WIDTH
    o_qi = o_v + A_KV_WIDTH
    o_ki = o_qi + IDX_WIDTH
    o_wi = o_ki + IDX_DIM
    o_sc = o_wi + IDX_HEADS
    o_rw = o_sc + 3 * SC_WIDTH
    o_gate = o_rw + RW_IN
    attn_scale = A_HEAD_DIM ** -0.5 * math.log2(math.e)
    idx_scale = (IDX_HEADS ** -0.5) * (IDX_DIM ** -0.5)
    head_sum = jnp.kron(jnp.eye(MXU_COLS // RW_HEAD_DIM, dtype=F32),
                        jnp.ones((RW_HEAD_DIM, RW_HEAD_DIM), F32)).astype(BF16)

    h = x
    for l in range(depth):
        w = w_in[l]
        pad = jnp.zeros((d, LANES - IDX_DIM - IDX_HEADS), F32)
        w_attn = jnp.concatenate(
            [w[:, o_q:o_k] * attn_scale, w[:, o_k:o_ki], w[:, o_ki:o_wi],
             w[:, o_wi:o_sc] * idx_scale, pad], axis=1).astype(BF16)
        w_sc = w[:, o_sc:o_rw].astype(BF16)
        w_rw = w[:, o_rw:o_gate].astype(BF16)
        w_gate = w[:, o_gate:].astype(BF16)
        g_mix = norm_mix[l][None, :]

        zero_lora = jnp.zeros((RW_W_LORA, RW_WIDTH), F32)
        w_lora = jnp.concatenate(
            [jnp.concatenate([rw_w_up[l], zero_lora], axis=1),
             jnp.concatenate([zero_lora, rw_a_up[l]], axis=1)], axis=0).astype(BF16)
        (q, k, v, qi, kw), y_b, (r, lw, kf, vv, kn, a, gg), gates = _in_proj(
            h, g_mix,
            (w_attn,) + tabs,
            (w_sc, sc_conv[l]),
            (w_rw, rw_mu[l][None, :], w_lora, rw_g_up[l].astype(BF16), rw_w0[l][None, :],
             rw_a0[l][None, :], rw_k_k[l][None, :], rw_k_a[l][None, :], head_sum),
            (w_gate, b_gate[l][None, :]), ts)

        y_a, y_c = _dsa_rwkv(
            q, k, v, qi, kw, k_sel, (r, lw, kf, vv, kn, a, gg),
            (rw_r_k[l].reshape(1, RW_WIDTH), rw_ln_w[l][None, :], rw_ln_b[l][None, :]))

        h = _merge_ffn(h, y_a, y_b, y_c, gates, w_branch[l].astype(BF16), w_out[l].astype(BF16),
                       norm_ffn[l][None, :], ffn_up[l].astype(BF16), ffn_conv[l],
                       ffn_down[l].astype(BF16), norm_final[None, :], l == depth - 1, ts)
    return h
```

```python
import functools
import math

import jax
import jax.numpy as jnp
from jax import lax
from jax.experimental import pallas as pl
from jax.experimental.pallas import tpu as pltpu

F32 = jnp.float32
BF16 = jnp.bfloat16

A_HEADS = 8
A_KV_HEADS = 2
A_HEAD_DIM = 64
A_WIDTH = A_HEADS * A_HEAD_DIM
A_KV_WIDTH = A_KV_HEADS * A_HEAD_DIM
IDX_HEADS = 8
IDX_DIM = 32
IDX_WIDTH = IDX_HEADS * IDX_DIM
TOPK_MAX = 256
ROPE_THETA = 500000.0
ROPE_FRACTION = 4
SC_WIDTH = 512
RW_HEADS = 8
RW_HEAD_DIM = 64
RW_WIDTH = RW_HEADS * RW_HEAD_DIM
RW_W_LORA = 64
RW_A_LORA = 64
RW_G_LORA = 128
RW_IN = 3 * RW_WIDTH + RW_W_LORA + RW_A_LORA + RW_G_LORA
RW_GN_EPS = 64e-5
RW_DECAY_SCALE = math.exp(-0.5)
N_BRANCH = 3
RMS_EPS = 1e-6

LANES = 128
MXU_COLS = 256
SUBLANES = 8
INT_MIN = -2147483648
VMEM_LIMIT = 48 * 1024 * 1024
WIDE_VMEM_LIMIT = 56 * 1024 * 1024

ROW_TILE = 512
DSA_Q_TILE = 256
RW_CHUNK = 64


def _cparams(sem):
    return pltpu.CompilerParams(dimension_semantics=sem, vmem_limit_bytes=VMEM_LIMIT)


def _rms(x, g):
    ms = jnp.mean(x * x, axis=-1, keepdims=True)
    return (x * lax.rsqrt(ms + RMS_EPS) * g).astype(BF16)


def _dot(a, b):
    return jnp.dot(a, b, preferred_element_type=F32)


def _dot_nt(a, b):
    return lax.dot_general(a, b, (((1,), (1,)), ((), ())), preferred_element_type=F32)


def _dot_tn(a, b):
    return lax.dot_general(a, b, (((0,), (0,)), ((), ())), preferred_element_type=F32)


def _shift_rows(x, prev, k):
    xr = pltpu.roll(x, k, 0)
    pr = pltpu.roll(prev, k, 0)
    rows = lax.broadcasted_iota(jnp.int32, pr.shape, 0)
    head = jnp.where(rows < k, pr, xr[0:SUBLANES])
    return jnp.concatenate([head, xr[SUBLANES:]], axis=0)


def _rope_slab(y, c, s1, s2, half):
    return y * c + pltpu.roll(y, LANES - half, 1) * s1 + pltpu.roll(y, half, 1) * s2


def _attn_proj_body(xn, w_ref, cq_ref, s1q_ref, s2q_ref, ci_ref, s1i_ref, s2i_ref,
                    q_ref, k_ref, v_ref, qi_ref, kw_ref):
    cq, s1q, s2q = cq_ref[0], s1q_ref[0], s2q_ref[0]
    ci, s1i, s2i = ci_ref[0], s1i_ref[0], s2i_ref[0]
    hq = A_HEAD_DIM // ROPE_FRACTION // 2
    hi = IDX_DIM // ROPE_FRACTION // 2
    col = 0
    for p in range(A_WIDTH // MXU_COLS):
        y = _dot(xn, w_ref[:, col:col + MXU_COLS])
        for half in range(MXU_COLS // LANES):
            lo = p * MXU_COLS + half * LANES
            q_ref[0, :, lo:lo + LANES] = _rope_slab(
                y[:, half * LANES:(half + 1) * LANES], cq, s1q, s2q, hq).astype(BF16)
        col += MXU_COLS
    y = _dot(xn, w_ref[:, col:col + 2 * LANES])
    k_ref[0] = _rope_slab(y[:, :LANES], cq, s1q, s2q, hq).astype(BF16)
    val = y[:, LANES:]
    swapped = pltpu.roll(val, A_HEAD_DIM, 1)
    low = lax.broadcasted_iota(jnp.int32, val.shape, 1) < A_HEAD_DIM
    slabs = (jnp.where(low, val, 1.0), jnp.where(low, 1.0, swapped),
             jnp.where(low, swapped, 1.0), jnp.where(low, 1.0, val))
    for i, slab in enumerate(slabs):
        v_ref[0, :, i * LANES:(i + 1) * LANES] = slab.astype(BF16)
    col += 2 * LANES
    y = _dot(xn, w_ref[:, col:col + IDX_WIDTH])
    for half in range(IDX_WIDTH // LANES):
        qi_ref[0, :, half * LANES:(half + 1) * LANES] = _rope_slab(
            y[:, half * LANES:(half + 1) * LANES], ci, s1i, s2i, hi).astype(BF16)
    col += IDX_WIDTH
    y = _dot(xn, w_ref[:, col:col + LANES])
    lane = lax.broadcasted_iota(jnp.int32, y.shape, 1)
    is_key = lane < IDX_DIM
    ck = jnp.where(is_key, ci, 1.0)
    kw_ref[0] = _rope_slab(y, ck, jnp.where(is_key, s1i, 0.0), jnp.where(is_key, s2i, 0.0), hi)


def _sc_proj_body(xn, w_ref, cw_ref, y_ref, carry_ref):
    ts = xn.shape[0]
    wd = MXU_COLS
    for lo in range(0, SC_WIDTH, wd):
        u = _dot(xn, w_ref[:, lo:lo + wd])
        gate_b = _dot(xn, w_ref[:, SC_WIDTH + lo:SC_WIDTH + lo + wd])
        gate_c = _dot(xn, w_ref[:, 2 * SC_WIDTH + lo:2 * SC_WIDTH + lo + wd])
        cu = gate_c * u
        prev = carry_ref[:, lo:lo + wd]
        conv = (_shift_rows(cu, prev, 2) * cw_ref[0:1, lo:lo + wd]
                + _shift_rows(cu, prev, 1) * cw_ref[1:2, lo:lo + wd]
                + cu * cw_ref[2:3, lo:lo + wd])
        y_ref[0, :, lo:lo + wd] = (gate_b * conv).astype(BF16)
        carry_ref[:, lo:lo + wd] = cu[ts - SUBLANES:ts]


def _split_dot(x, m):
    hi = x.astype(BF16)
    lo = (x - hi.astype(F32)).astype(BF16)
    return _dot(hi, m) + _dot(lo, m)


def _rw_proj_body(xn, w_ref, mu_ref, wlora_ref, gup_ref, w0_ref, a0_ref, kk_ref, ka_ref, hsum_ref,
                  r_ref, lw_ref, k_ref, v_ref, kn_ref, a_ref, gg_ref, carry_ref):
    ts = xn.shape[0]

    def mixed(lo, width):
        z = _dot(xn, w_ref[:, lo:lo + width])
        zp = _shift_rows(z, carry_ref[:, lo:lo + width], 1)
        carry_ref[:, lo:lo + width] = z[ts - SUBLANES:ts]
        return z + (zp - z) * mu_ref[:, lo:lo + width]

    lora_lo = 3 * RW_WIDTH
    zlg = mixed(lora_lo, RW_IN - lora_lo)
    zl = zlg[:, :LANES]
    lane = lax.broadcasted_iota(jnp.int32, zl.shape, 1)
    zl = jnp.where(lane < RW_W_LORA, jnp.tanh(zl), zl).astype(BF16)
    zg = jax.nn.sigmoid(zlg[:, LANES:]).astype(BF16)
    wd = MXU_COLS
    for lo in range(0, RW_WIDTH, wd):
        sl = slice(lo, lo + wd)
        up = _dot(zl, wlora_ref[:, lo:lo + wd])
        ua = _dot(zl, wlora_ref[:, RW_WIDTH + lo:RW_WIDTH + lo + wd])
        lw_ref[0, :, sl] = -RW_DECAY_SCALE * jax.nn.sigmoid(w0_ref[:, sl] + up)
        a = jax.nn.sigmoid(a0_ref[:, sl] + ua)
        a_ref[0, :, sl] = a.astype(BF16)
        gg_ref[0, :, sl] = _dot(zg, gup_ref[:, sl]).astype(BF16)
        r_ref[0, :, sl] = mixed(lo, wd).astype(BF16)
        k = mixed(RW_WIDTH + lo, wd)
        v_ref[0, :, sl] = mixed(2 * RW_WIDTH + lo, wd).astype(BF16)
        kk = k * kk_ref[:, sl]
        ss = _split_dot(kk * kk, hsum_ref[...])
        kn_ref[0, :, sl] = (kk * lax.rsqrt(jnp.maximum(ss, 1e-24))).astype(BF16)
        k_ref[0, :, sl] = (k * (1.0 + (a - 1.0) * ka_ref[:, sl])).astype(BF16)


def _gate_proj_body(xn, w_ref, b_ref, o_ref):
    n = w_ref.shape[1]
    step = 4 * LANES
    for lo in range(0, n, step):
        y = _dot(xn, w_ref[:, lo:lo + step]) + b_ref[:, lo:lo + step]
        o_ref[0, :, lo:lo + step] = jax.nn.sigmoid(y).astype(BF16)


N_ATTN_IN, N_SC_IN, N_RW_IN, N_GATE_IN = 7, 2, 9, 2
N_ATTN_OUT, N_SC_OUT, N_RW_OUT = 5, 1, 7


def _in_proj_kernel(*refs):
    refs = list(refs)

    def take(n):
        taken, refs[:] = refs[:n], refs[n:]
        return taken

    h_ref, g_ref = take(2)
    attn_in, sc_in, rw_in, gate_in = (take(n) for n in (N_ATTN_IN, N_SC_IN, N_RW_IN, N_GATE_IN))
    attn_out, sc_out, rw_out = (take(n) for n in (N_ATTN_OUT, N_SC_OUT, N_RW_OUT))
    gate_out, sc_carry, rw_carry = take(3)

    @pl.when(pl.program_id(1) == 0)
    def _():
        sc_carry[...] = jnp.zeros_like(sc_carry)
        rw_carry[...] = jnp.zeros_like(rw_carry)

    xn = _rms(h_ref[0], g_ref[...])
    _attn_proj_body(xn, *attn_in, *attn_out)
    _sc_proj_body(xn, *sc_in, *sc_out, sc_carry)
    _rw_proj_body(xn, *rw_in, *rw_out, rw_carry)
    _gate_proj_body(xn, *gate_in, gate_out)


def _in_proj(h, g, attn_in, sc_in, rw_in, gate_in, ts):
    b, s, d = h.shape
    row = lambda i, j: (i, j, 0)
    const = lambda i, j: (0, 0)

    def seq_spec(width):
        return pl.BlockSpec((1, ts, width), row)

    def whole(x):
        if x.size * x.dtype.itemsize >= (1 << 20):
            return pl.BlockSpec(x.shape, const, pipeline_mode=pl.Buffered(1))
        return pl.BlockSpec(x.shape, const)

    w_attn, tabs = attn_in[0], attn_in[1:]
    in_specs = ([seq_spec(d), whole(g), whole(w_attn)] + [seq_spec(LANES)] * len(tabs)
                + [whole(x) for x in sc_in] + [whole(x) for x in rw_in]
                + [whole(x) for x in gate_in])
    out_widths = ([(A_WIDTH, BF16), (LANES, BF16), (4 * LANES, BF16), (IDX_WIDTH, BF16),
                   (LANES, F32), (SC_WIDTH, BF16)]
                  + [(RW_WIDTH, F32 if i == 1 else BF16) for i in range(N_RW_OUT)]
                  + [(gate_in[0].shape[1], BF16)])
    outs = pl.pallas_call(
        _in_proj_kernel,
        grid=(b, s // ts),
        in_specs=in_specs,
        out_specs=[seq_spec(w) for w, _ in out_widths],
        out_shape=[jax.ShapeDtypeStruct((b, s, w), dt) for w, dt in out_widths],
        scratch_shapes=[pltpu.VMEM((SUBLANES, SC_WIDTH), F32), pltpu.VMEM((SUBLANES, RW_IN), F32)],
        compiler_params=pltpu.CompilerParams(dimension_semantics=("arbitrary", "arbitrary"),
                                             vmem_limit_bytes=WIDE_VMEM_LIMIT),
        name="in_proj",
    )(h, g, *attn_in, *sc_in, *rw_in, *gate_in)
    return outs[:5], outs[5], outs[6:13], outs[13]


def _dsa_topk_bias(qi_ref, kwk_ref, kwq_ref, key_ref, hi_ref, lo_ref, bias_ref, *, t0, k_sel,
                   side_work):
    tq, sk = bias_ref.shape
    half_min = -(1 << 15)

    ki = kwk_ref[0][:, 0:IDX_DIM].astype(BF16)
    wq = kwq_ref[0]
    qi = qi_ref[0]
    isc = jnp.zeros((tq, sk), F32)
    for h in range(IDX_HEADS):
        rel = jnp.maximum(_dot_nt(qi[:, h * IDX_DIM:(h + 1) * IDX_DIM], ki), 0.0)
        isc = isc + wq[:, IDX_DIM + h:IDX_DIM + h + 1] * rel

    cols = lax.broadcasted_iota(jnp.int32, (tq, sk), 1)
    rows = t0 + lax.broadcasted_iota(jnp.int32, (tq, sk), 0)
    bits = pltpu.bitcast(isc, jnp.int32)
    key = bits ^ ((bits >> 31) & jnp.int32(0x7FFFFFFF))
    key = jnp.where(cols <= rows, key, INT_MIN)
    key_ref[...] = key
    hi_ref[...] = (key >> 16).astype(jnp.int16)
    lo_ref[...] = ((key & jnp.int32(0xFFFF)) + half_min).astype(jnp.int16)


    def search_half(src_ref, cnt0):
        tau_u, cnt_tau = jnp.zeros((tq, 1), jnp.int32), cnt0
        for it in range(16):
            cand_u = tau_u | (1 << (15 - it))
            cand = jnp.broadcast_to(cand_u + half_min, (tq, LANES)).astype(jnp.int16)
            accs = [jnp.zeros((tq, LANES), jnp.int16) for _ in range(4)]
            for j in range(sk // LANES):
                hit = src_ref[:, j * LANES:(j + 1) * LANES] >= cand
                accs[j % 4] = accs[j % 4] + jnp.where(hit, jnp.int16(1), jnp.int16(0))
            acc = (accs[0] + accs[1]) + (accs[2] + accs[3])
            cnt = jnp.sum(acc.astype(F32), axis=1, keepdims=True)
            ok = cnt >= k_sel
            tau_u, cnt_tau = jnp.where(ok, cand_u, tau_u), jnp.where(ok, cnt, cnt_tau)
            next(side_work, None)
        return tau_u, cnt_tau

    tau_hi_u, cnt_hi = search_half(hi_ref, jnp.full((tq, 1), float(sk), F32))
    tau_hi = tau_hi_u + half_min
    t_hi = jnp.broadcast_to(tau_hi, (tq, LANES)).astype(jnp.int16)
    for j in range(sk // LANES):
        sl = slice(j * LANES, (j + 1) * LANES)
        hi_j = hi_ref[:, sl]
        inside = jnp.where(hi_j == t_hi, lo_ref[:, sl], jnp.int16(half_min))
        lo_ref[:, sl] = jnp.where(hi_j > t_hi, jnp.int16(-half_min - 1), inside)
    tau_lo_u, cnt_ge = search_half(lo_ref, cnt_hi)
    tau = (tau_hi << 16) | tau_lo_u

    drop = jnp.where(tau == INT_MIN, float(2 * sk), cnt_ge - k_sel)
    r2 = lax.broadcasted_iota(jnp.int32, (LANES, 2 * LANES), 0)
    c2 = lax.broadcasted_iota(jnp.int32, (LANES, 2 * LANES), 1)
    tri = jnp.where((r2 > c2) | (c2 >= LANES), 1.0, 0.0).astype(BF16)
    after = jnp.zeros((tq, LANES), F32)
    for j in reversed(range(sk // LANES)):
        sl = slice(j * LANES, (j + 1) * LANES)
        key_j = key_ref[:, sl]
        eq = key_j == tau
        pr = _dot(jnp.where(eq, 1.0, 0.0).astype(BF16), tri)
        tied = jnp.where(eq, jnp.where(after + pr[:, :LANES] >= drop, 0.0, -jnp.inf), -jnp.inf)
        bias_ref[:, sl] = jnp.where(key_j > tau, 0.0, tied)
        after = after + pr[:, LANES:]


N_RW_STREAMS = 7
N_RW_REFS = N_RW_STREAMS + 4


def _dsa_rwkv_kernel(*refs, first_block, k_sel, rw_chunk):
    q_ref, k_ref, v_ref, qi_ref, kwk_ref, kwq_ref = refs[:6]
    rw_in = refs[6:6 + N_RW_REFS]
    o_ref, y_ref, st_out_ref, key_ref, hi_ref, lo_ref, bias_ref = refs[6 + N_RW_REFS + 2:]
    rwkv = _rwkv_stages(*rw_in, y_ref, st_out_ref, c=rw_chunk)
    tq = q_ref.shape[1]
    sk = k_ref.shape[1]
    t0 = first_block * tq
    if sk <= k_sel:
        cols = lax.broadcasted_iota(jnp.int32, (tq, sk), 1)
        rows = t0 + lax.broadcasted_iota(jnp.int32, (tq, sk), 0)
        bias_ref[...] = jnp.where(cols <= rows, 0.0, -jnp.inf)
    else:
        _dsa_topk_bias(qi_ref, kwk_ref, kwq_ref, key_ref, hi_ref, lo_ref, bias_ref,
                       t0=t0, k_sel=k_sel, side_work=rwkv)
    for _ in rwkv:
        pass

    group = A_HEADS // A_KV_HEADS
    low = lax.broadcasted_iota(jnp.int32, (tq, LANES), 1) < A_HEAD_DIM
    bias = bias_ref[...]
    kk = k_ref[0]
    for h2 in range(A_HEADS // 2):
        res = []
        for h in (2 * h2, 2 * h2 + 1):
            n = h // group
            kn = kk[:, n * A_HEAD_DIM:(n + 1) * A_HEAD_DIM]
            s = _dot_nt(q_ref[0, :, h * A_HEAD_DIM:(h + 1) * A_HEAD_DIM], kn) + bias
            m = jnp.max(s, axis=1, keepdims=True)
            p = jnp.exp2(s - m).astype(BF16)
            vx = v_ref[0, :, (2 * n + h % 2) * LANES:(2 * n + h % 2 + 1) * LANES]
            oe = _dot(p, vx)
            res.append(oe / pltpu.roll(oe, A_HEAD_DIM, 1))
        o_ref[0, :, h2 * LANES:(h2 + 1) * LANES] = jnp.where(low, res[0], res[1]).astype(BF16)


def _dsa_rwkv(q, k, v, qi, kw, k_sel, rw_streams, rw_vecs):
    b, s, _ = q.shape
    tq = min(DSA_Q_TILE, s)
    w = rw_streams[0].shape[2]
    c = min(RW_CHUNK, s)
    state = jnp.zeros((b, w // LANES, LANES, LANES), F32)
    ya = jnp.zeros((b, s, A_WIDTH), BF16)
    yc = jnp.zeros((b, s, w), BF16)
    n_in = 6 + N_RW_REFS
    for first in range(s // tq):
        sk = (first + 1) * tq
        qmap = lambda i, j, first=first: (i, first, 0)
        kmap = lambda i, j: (i, 0, 0)
        st_spec = pl.BlockSpec((1, w // LANES, LANES, LANES), lambda i, j: (i, 0, 0, 0))
        ya, yc, state = pl.pallas_call(
            functools.partial(_dsa_rwkv_kernel, first_block=first, k_sel=k_sel, rw_chunk=c),
            grid=(b, 1),
            in_specs=[pl.BlockSpec((1, tq, A_WIDTH), qmap), pl.BlockSpec((1, sk, LANES), kmap),
                      pl.BlockSpec((1, sk, 4 * LANES), kmap),
                      pl.BlockSpec((1, tq, IDX_WIDTH), qmap),
                      pl.BlockSpec((1, sk, LANES), kmap), pl.BlockSpec((1, tq, LANES), qmap)]
                     + [pl.BlockSpec((1, tq, w), qmap)] * N_RW_STREAMS
                     + [pl.BlockSpec((1, w), lambda i, j: (0, 0))] * 3 + [st_spec]
                     + [pl.BlockSpec(memory_space=pl.ANY)] * 2,
            out_specs=[pl.BlockSpec((1, tq, A_WIDTH), qmap), pl.BlockSpec((1, tq, w), qmap),
                       st_spec],
            out_shape=[jax.ShapeDtypeStruct(ya.shape, BF16), jax.ShapeDtypeStruct(yc.shape, BF16),
                       jax.ShapeDtypeStruct(state.shape, F32)],
            input_output_aliases={n_in: 0, n_in + 1: 1},
            scratch_shapes=[pltpu.VMEM((tq, sk), jnp.int32), pltpu.VMEM((tq, sk), jnp.int16),
                            pltpu.VMEM((tq, sk), jnp.int16), pltpu.VMEM((tq, sk), F32)],
            compiler_params=_cparams(("arbitrary", "arbitrary")),
            name=f"dsa_rwkv_{sk}",
        )(q, k, v, qi, kw, kw, *rw_streams, *rw_vecs, state, ya, yc)
    return ya, yc


def _rwkv_stages(r_ref, lw_ref, k_ref, v_ref, kn_ref, a_ref, g_ref, rk_ref, lnw_ref, lnb_ref,
                 st_in_ref, y_ref, st_out_ref, *, c):
    width = r_ref.shape[2]
    nc = r_ref.shape[1] // c
    hd = RW_HEAD_DIM

    lw_all = lw_ref[0]
    rows = lax.broadcasted_iota(jnp.int32, lw_all.shape, 0) & (c - 1)
    cum_all = lw_all
    step = 1
    while step < c:
        cum_all = cum_all + jnp.where(rows >= step, pltpu.roll(cum_all, step, 0), 0.0)
        step *= 2

    lane = lax.broadcasted_iota(jnp.int32, (c, LANES), 1)
    first = lane < hd
    r2 = lax.broadcasted_iota(jnp.int32, (2 * c, 2 * c), 0)
    c2 = lax.broadcasted_iota(jnp.int32, (2 * c, 2 * c), 1)
    same_head = (r2 >= c) == (c2 >= c)
    t_row = r2 & (c - 1)
    t_col = c2 & (c - 1)
    strict = same_head & (t_col < t_row)
    lower = same_head & (t_col <= t_row)

    def blk(x):
        return jnp.concatenate([jnp.where(first, x, 0.0), jnp.where(first, 0.0, x)],
                               axis=0).astype(BF16)

    def head_sum(x):
        s0 = jnp.sum(jnp.where(first, x, 0.0), axis=1, keepdims=True)
        s1 = jnp.sum(jnp.where(first, 0.0, x), axis=1, keepdims=True)
        return jnp.where(first, s0, s1)

    pairs = range(width // LANES)
    sls = [slice(p * LANES, (p + 1) * LANES) for p in pairs]
    units = [(j, p) for j in range(nc) for p in pairs]
    n = 2 * c
    la, lr, vb, bk_t, bk_h, p_last, vals, bonus_in = {}, {}, {}, {}, {}, {}, {}, {}
    for j in range(nc):
        rs = slice(j * c, (j + 1) * c)
        r, kf, v = (x[0, rs, :].astype(F32) for x in (r_ref, k_ref, v_ref))
        kn, a = kn_ref[0, rs, :].astype(F32), a_ref[0, rs, :].astype(F32)
        lw, cum = lw_all[rs], cum_all[rs]
        cum_last = cum[c - 1:c, :]
        p_inv = jnp.exp(-cum)
        p_rem = jnp.exp(cum_last - cum)
        kb = kn * a
        a_t = -kn * jnp.exp(cum - lw)
        r_t = r * jnp.exp(cum)
        b_t = kb * p_inv
        k_t = kf * p_inv
        b_h = kb * p_rem
        k_h = kf * p_rem
        p_last[j] = jnp.exp(cum_last)
        vals[j] = v
        bonus_in[j] = r * kf * rk_ref[...]
        for p in pairs:
            sl = sls[p]
            la[j, p], lr[j, p], vb[j, p] = blk(a_t[:, sl]), blk(r_t[:, sl]), blk(v[:, sl])
            bk_t[j, p] = jnp.concatenate([blk(b_t[:, sl]), blk(k_t[:, sl])], axis=0)
            bk_h[j, p] = jnp.concatenate([blk(b_h[:, sl]), blk(k_h[:, sl])], axis=0)
        yield
    gram = {u: _dot_nt(jnp.concatenate([la[u], lr[u]], axis=0), bk_t[u]) for u in units}
    x = {u: jnp.where(strict, gram[u][:n, :n], 0.0) for u in units}
    npow = dict(x)
    yield
    for _ in range(int(math.log2(c)) - 1):
        npow_b = {u: npow[u].astype(BF16) for u in units}
        npow = {u: _dot(npow_b[u], npow_b[u]) for u in units}
        x = {u: x[u] + npow[u] + _dot(x[u].astype(BF16), npow[u].astype(BF16)) for u in units}
        yield
    x_b = {u: x[u].astype(BF16) for u in units}
    a_ak = {u: jnp.where(strict, gram[u][:n, n:], 0.0).astype(BF16) for u in units}
    lower2 = jnp.concatenate([lower, lower], axis=1)
    a_rbk = {u: jnp.where(lower2, gram[u][n:, :], 0.0).astype(BF16) for u in units}
    akv = {u: _dot(a_ak[u], vb[u]) for u in units}
    yield

    st = [st_in_ref[0, p] for p in pairs]
    for j in range(nc):
        st_b = [s.astype(BF16) for s in st]
        rhs = [_dot_nt(la[j, p], st_b[p]) + akv[j, p] for p in pairs]
        sa = [rhs[p] + _dot(x_b[j, p], rhs[p].astype(BF16)) for p in pairs]
        yield
        sav = [jnp.concatenate([sa[p].astype(BF16), vb[j, p]], axis=0) for p in pairs]
        yb = [_dot_nt(lr[j, p], st_b[p]) + _dot(a_rbk[j, p], sav[p]) for p in pairs]
        st = [st[p] * p_last[j][:, sls[p]] + _dot_tn(sav[p], bk_h[j, p]) for p in pairs]
        yield
        rs = slice(j * c, (j + 1) * c)
        for p in pairs:
            sl = sls[p]
            y = yb[p][:c] + yb[p][c:]
            mean = head_sum(y) * (1.0 / hd)
            yc = y - mean
            var = head_sum(yc * yc) * (1.0 / hd)
            yn = yc * lax.rsqrt(var + RW_GN_EPS) * lnw_ref[:, sl] + lnb_ref[:, sl]
            bonus = head_sum(bonus_in[j][:, sl]) * vals[j][:, sl]
            y_ref[0, rs, sl] = ((yn + bonus) * g_ref[0, rs, sl]).astype(BF16)
        yield
    for p in pairs:
        st_out_ref[0, p] = st[p]


def _merge_ffn_kernel(h_ref, ya_ref, yb_ref, yc_ref, gt_ref, wb_ref, wo_ref,
                      g_ref, wup_ref, cw_ref, wd_ref, gf_ref, o_ref, act_ref, carry_ref, *,
                      final_norm):
    @pl.when(pl.program_id(1) == 0)
    def _():
        carry_ref[...] = jnp.zeros_like(carry_ref)

    ts, d = h_ref.shape[1], h_ref.shape[2]
    dff = wd_ref.shape[0]
    mixed = None
    for i, y_ref in enumerate((ya_ref, yb_ref, yc_ref)):
        up = _dot(y_ref[0], wb_ref[i])
        term = gt_ref[0, :, i * d:(i + 1) * d].astype(F32) * up
        mixed = term if mixed is None else mixed + term
    h_mid = h_ref[0] + _dot(mixed.astype(BF16), wo_ref[...])
    xn = _rms(h_mid, g_ref[...])

    def conv(sl):
        u = _dot(xn, wup_ref[:, sl])
        prev = carry_ref[:, sl]
        out = (_shift_rows(u, prev, 2) * cw_ref[0:1, sl] + _shift_rows(u, prev, 1) * cw_ref[1:2, sl]
               + u * cw_ref[2:3, sl])
        carry_ref[:, sl] = u[ts - SUBLANES:ts]
        return out

    for lo in range(0, dff, MXU_COLS):
        gate = conv(slice(lo, lo + MXU_COLS))
        up = conv(slice(dff + lo, dff + lo + MXU_COLS))
        act_ref[:, lo:lo + MXU_COLS] = (gate * jax.nn.sigmoid(gate) * up).astype(BF16)
    out = h_mid + _dot(act_ref[...], wd_ref[...])
    if final_norm:
        ms = jnp.mean(out * out, axis=-1, keepdims=True)
        out = out * lax.rsqrt(ms + RMS_EPS) * gf_ref[...]
    o_ref[0] = out


def _merge_ffn(h, ya, yb, yc, gates, wb, wo, g, w_up, w_conv, w_down, g_final, final_norm, ts):
    b, s, d = h.shape
    dff = w_down.shape[0]
    bw = ya.shape[2]
    row = lambda i, j: (i, j, 0)
    const = lambda i, j: (0, 0)
    resident = dict(pipeline_mode=pl.Buffered(1))
    return pl.pallas_call(
        functools.partial(_merge_ffn_kernel, final_norm=final_norm),
        grid=(b, s // ts),
        in_specs=[pl.BlockSpec((1, ts, d), row)] + [pl.BlockSpec((1, ts, bw), row)] * 3
                 + [pl.BlockSpec((1, ts, N_BRANCH * d), row),
                    pl.BlockSpec((N_BRANCH, bw, d), lambda i, j: (0, 0, 0), **resident),
                    pl.BlockSpec((d, d), const, **resident),
                    pl.BlockSpec((1, d), const),
                    pl.BlockSpec((d, 2 * dff), const, **resident),
                    pl.BlockSpec((3, 2 * dff), const),
                    pl.BlockSpec((dff, d), const, **resident),
                    pl.BlockSpec((1, d), const)],
        out_specs=pl.BlockSpec((1, ts, d), row),
        out_shape=jax.ShapeDtypeStruct((b, s, d), F32),
        scratch_shapes=[pltpu.VMEM((ts, dff), BF16), pltpu.VMEM((SUBLANES, 2 * dff), F32)],
        compiler_params=pltpu.CompilerParams(dimension_semantics=("arbitrary", "arbitrary"),
                                             vmem_limit_bytes=WIDE_VMEM_LIMIT),
        name="merge_ffn",
    )(h, ya, yb, yc, gates, wb, wo, g, w_up, w_conv, w_down, g_final)


def _rope_tables(positions, head_dim):
    rot = head_dim // ROPE_FRACTION
    half = rot // 2
    inv = ROPE_THETA ** (-jnp.arange(half, dtype=F32) * 2.0 / rot)
    ang = positions.astype(F32)[:, :, None] * inv
    cos, sin = jnp.cos(ang), jnp.sin(ang)
    zeros = jnp.zeros_like(cos)
    rest = head_dim - rot
    pad1 = jnp.ones(cos.shape[:2] + (rest,), F32)
    pad0 = jnp.zeros(cos.shape[:2] + (rest,), F32)
    reps = LANES // head_dim
    c = jnp.tile(jnp.concatenate([cos, cos, pad1], axis=-1), (1, 1, reps))
    s1 = jnp.tile(jnp.concatenate([-sin, zeros, pad0], axis=-1), (1, 1, reps))
    s2 = jnp.tile(jnp.concatenate([zeros, sin, pad0], axis=-1), (1, 1, reps))
    return c, s1, s2


def kernel(x, positions, norm_mix, w_in, b_gate, sc_conv, rw_mu, rw_w0, rw_w_up, rw_a0, rw_a_up, rw_g_up, rw_k_k, rw_k_a, rw_r_k, rw_ln_w, rw_ln_b, w_branch, w_out, norm_ffn, ffn_up, ffn_conv, ffn_down, norm_final):
    bsz, seq, d = x.shape
    depth = w_in.shape[0]
    ts = min(ROW_TILE, seq)
    k_sel = min(TOPK_MAX, seq // 4)

    tabs = _rope_tables(positions, A_HEAD_DIM) + _rope_tables(positions, IDX_DIM)
    o_q = 0
    o_k = o_q + A_WIDTH
    o_v = o_k + A_KV_WIDTH
    o_qi = o_v + A_KV_WIDTH
    o_ki = o_qi + IDX_WIDTH
    o_wi = o_ki + IDX_DIM
    o_sc = o_wi + IDX_HEADS
    o_rw = o_sc + 3 * SC_WIDTH
    o_gate = o_rw + RW_IN
    attn_scale = A_HEAD_DIM ** -0.5 * math.log2(math.e)
    idx_scale = (IDX_HEADS ** -0.5) * (IDX_DIM ** -0.5)
    head_sum = jnp.kron(jnp.eye(MXU_COLS // RW_HEAD_DIM, dtype=F32),
                        jnp.ones((RW_HEAD_DIM, RW_HEAD_DIM), F32)).astype(BF16)

    h = x
    for l in range(depth):
        w = w_in[l]
        pad = jnp.zeros((d, LANES - IDX_DIM - IDX_HEADS), F32)
        w_attn = jnp.concatenate(
            [w[:, o_q:o_k] * attn_scale, w[:, o_k:o_ki], w[:, o_ki:o_wi],
             w[:, o_wi:o_sc] * idx_scale, pad], axis=1).astype(BF16)
        w_sc = w[:, o_sc:o_rw].astype(BF16)
        w_rw = w[:, o_rw:o_gate].astype(BF16)
        w_gate = w[:, o_gate:].astype(BF16)
        g_mix = norm_mix[l][None, :]

        zero_lora = jnp.zeros((RW_W_LORA, RW_WIDTH), F32)
        w_lora = jnp.concatenate(
            [jnp.concatenate([rw_w_up[l], zero_lora], axis=1),
             jnp.concatenate([zero_lora, rw_a_up[l]], axis=1)], axis=0).astype(BF16)
        (q, k, v, qi, kw), y_b, (r, lw, kf, vv, kn, a, gg), gates = _in_proj(
            h, g_mix,
            (w_attn,) + tabs,
            (w_sc, sc_conv[l]),
            (w_rw, rw_mu[l][None, :], w_lora, rw_g_up[l].astype(BF16), rw_w0[l][None, :],
             rw_a0[l][None, :], rw_k_k[l][None, :], rw_k_a[l][None, :], head_sum),
            (w_gate, b_gate[l][None, :]), ts)

        y_a, y_c = _dsa_rwkv(
            q, k, v, qi, kw, k_sel, (r, lw, kf, vv, kn, a, gg),
            (rw_r_k[l].reshape(1, RW_WIDTH), rw_ln_w[l][None, :], rw_ln_b[l][None, :]))

        h = _merge_ffn(h, y_a, y_b, y_c, gates, w_branch[l].astype(BF16), w_out[l].astype(BF16),
                       norm_ffn[l][None, :], ffn_up[l].astype(BF16), ffn_conv[l],
                       ffn_down[l].astype(BF16), norm_final[None, :], l == depth - 1, ts)
    return h
```

```python
import functools
import math

import jax
import jax.numpy as jnp
from jax import lax
from jax.experimental import pallas as pl
from jax.experimental.pallas import tpu as pltpu

F32 = jnp.float32
BF16 = jnp.bfloat16

A_HEADS = 8
A_KV_HEADS = 2
A_HEAD_DIM = 64
A_WIDTH = A_HEADS * A_HEAD_DIM
A_KV_WIDTH = A_KV_HEADS * A_HEAD_DIM
IDX_HEADS = 8
IDX_DIM = 32
IDX_WIDTH = IDX_HEADS * IDX_DIM
TOPK_MAX = 256
ROPE_THETA = 500000.0
ROPE_FRACTION = 4
SC_WIDTH = 512
RW_HEADS = 8
RW_HEAD_DIM = 64
RW_WIDTH = RW_HEADS * RW_HEAD_DIM
RW_W_LORA = 64
RW_A_LORA = 64
RW_G_LORA = 128
RW_IN = 3 * RW_WIDTH + RW_W_LORA + RW_A_LORA + RW_G_LORA
RW_GN_EPS = 64e-5
RW_DECAY_SCALE = math.exp(-0.5)
N_BRANCH = 3
RMS_EPS = 1e-6

LANES = 128
MXU_COLS = 256
SUBLANES = 8
INT_MIN = -2147483648
VMEM_LIMIT = 48 * 1024 * 1024
WIDE_VMEM_LIMIT = 56 * 1024 * 1024

ROW_TILE = 512
DSA_Q_TILE = 256
RW_CHUNK = 64


def _cparams(sem):
    return pltpu.CompilerParams(dimension_semantics=sem, vmem_limit_bytes=VMEM_LIMIT)


def _rms(x, g):
    ms = jnp.mean(x * x, axis=-1, keepdims=True)
    return (x * lax.rsqrt(ms + RMS_EPS) * g).astype(BF16)


def _dot(a, b):
    return jnp.dot(a, b, preferred_element_type=F32)


def _dot_nt(a, b):
    return lax.dot_general(a, b, (((1,), (1,)), ((), ())), preferred_element_type=F32)


def _dot_tn(a, b):
    return lax.dot_general(a, b, (((0,), (0,)), ((), ())), preferred_element_type=F32)


def _shift_rows(x, prev, k):
    xr = pltpu.roll(x, k, 0)
    pr = pltpu.roll(prev, k, 0)
    rows = lax.broadcasted_iota(jnp.int32, pr.shape, 0)
    head = jnp.where(rows < k, pr, xr[0:SUBLANES])
    return jnp.concatenate([head, xr[SUBLANES:]], axis=0)


def _rope_slab(y, c, s1, s2, half):
    return y * c + pltpu.roll(y, LANES - half, 1) * s1 + pltpu.roll(y, half, 1) * s2


def _attn_proj_body(xn, w_ref, cq_ref, s1q_ref, s2q_ref, ci_ref, s1i_ref, s2i_ref,
                    q_ref, k_ref, v_ref, qi_ref, kw_ref):
    cq, s1q, s2q = cq_ref[0], s1q_ref[0], s2q_ref[0]
    ci, s1i, s2i = ci_ref[0], s1i_ref[0], s2i_ref[0]
    hq = A_HEAD_DIM // ROPE_FRACTION // 2
    hi = IDX_DIM // ROPE_FRACTION // 2
    col = 0
    for p in range(A_WIDTH // MXU_COLS):
        y = _dot(xn, w_ref[:, col:col + MXU_COLS])
        for half in range(MXU_COLS // LANES):
            lo = p * MXU_COLS + half * LANES
            q_ref[0, :, lo:lo + LANES] = _rope_slab(
                y[:, half * LANES:(half + 1) * LANES], cq, s1q, s2q, hq).astype(BF16)
        col += MXU_COLS
    y = _dot(xn, w_ref[:, col:col + 2 * LANES])
    k_ref[0] = _rope_slab(y[:, :LANES], cq, s1q, s2q, hq).astype(BF16)
    val = y[:, LANES:]
    swapped = pltpu.roll(val, A_HEAD_DIM, 1)
    low = lax.broadcasted_iota(jnp.int32, val.shape, 1) < A_HEAD_DIM
    slabs = (jnp.where(low, val, 1.0), jnp.where(low, 1.0, swapped),
             jnp.where(low, swapped, 1.0), jnp.where(low, 1.0, val))
    for i, slab in enumerate(slabs):
        v_ref[0, :, i * LANES:(i + 1) * LANES] = slab.astype(BF16)
    col += 2 * LANES
    y = _dot(xn, w_ref[:, col:col + IDX_WIDTH])
    for half in range(IDX_WIDTH // LANES):
        qi_ref[0, :, half * LANES:(half + 1) * LANES] = _rope_slab(
            y[:, half * LANES:(half + 1) * LANES], ci, s1i, s2i, hi).astype(BF16)
    col += IDX_WIDTH
    y = _dot(xn, w_ref[:, col:col + LANES])
    lane = lax.broadcasted_iota(jnp.int32, y.shape, 1)
    is_key = lane < IDX_DIM
    ck = jnp.where(is_key, ci, 1.0)
    kw_ref[0] = _rope_slab(y, ck, jnp.where(is_key, s1i, 0.0), jnp.where(is_key, s2i, 0.0), hi)


def _sc_proj_body(xn, w_ref, cw_ref, y_ref, carry_ref):
    ts = xn.shape[0]
    wd = MXU_COLS
    for lo in range(0, SC_WIDTH, wd):
        u = _dot(xn, w_ref[:, lo:lo + wd])
        gate_b = _dot(xn, w_ref[:, SC_WIDTH + lo:SC_WIDTH + lo + wd])
        gate_c = _dot(xn, w_ref[:, 2 * SC_WIDTH + lo:2 * SC_WIDTH + lo + wd])
        cu = gate_c * u
        prev = carry_ref[:, lo:lo + wd]
        conv = (_shift_rows(cu, prev, 2) * cw_ref[0:1, lo:lo + wd]
                + _shift_rows(cu, prev, 1) * cw_ref[1:2, lo:lo + wd]
                + cu * cw_ref[2:3, lo:lo + wd])
        y_ref[0, :, lo:lo + wd] = (gate_b * conv).astype(BF16)
        carry_ref[:, lo:lo + wd] = cu[ts - SUBLANES:ts]


def _split_dot(x, m):
    hi = x.astype(BF16)
    lo = (x - hi.astype(F32)).astype(BF16)
    return _dot(hi, m) + _dot(lo, m)


def _rw_proj_body(xn, w_ref, mu_ref, wlora_ref, gup_ref, w0_ref, a0_ref, kk_ref, ka_ref, hsum_ref,
                  r_ref, lw_ref, k_ref, v_ref, kn_ref, a_ref, gg_ref, carry_ref):
    ts = xn.shape[0]

    def mixed(lo, width):
        z = _dot(xn, w_ref[:, lo:lo + width])
        zp = _shift_rows(z, carry_ref[:, lo:lo + width], 1)
        carry_ref[:, lo:lo + width] = z[ts - SUBLANES:ts]
        return z + (zp - z) * mu_ref[:, lo:lo + width]

    lora_lo = 3 * RW_WIDTH
    zlg = mixed(lora_lo, RW_IN - lora_lo)
    zl = zlg[:, :LANES]
    lane = lax.broadcasted_iota(jnp.int32, zl.shape, 1)
    zl = jnp.where(lane < RW_W_LORA, jnp.tanh(zl), zl).astype(BF16)
    zg = jax.nn.sigmoid(zlg[:, LANES:]).astype(BF16)
    wd = MXU_COLS
    for lo in range(0, RW_WIDTH, wd):
        sl = slice(lo, lo + wd)
        up = _dot(zl, wlora_ref[:, lo:lo + wd])
        ua = _dot(zl, wlora_ref[:, RW_WIDTH + lo:RW_WIDTH + lo + wd])
        lw_ref[0, :, sl] = -RW_DECAY_SCALE * jax.nn.sigmoid(w0_ref[:, sl] + up)
        a = jax.nn.sigmoid(a0_ref[:, sl] + ua)
        a_ref[0, :, sl] = a.astype(BF16)
        gg_ref[0, :, sl] = _dot(zg, gup_ref[:, sl]).astype(BF16)
        r_ref[0, :, sl] = mixed(lo, wd).astype(BF16)
        k = mixed(RW_WIDTH + lo, wd)
        v_ref[0, :, sl] = mixed(2 * RW_WIDTH + lo, wd).astype(BF16)
        kk = k * kk_ref[:, sl]
        ss = _split_dot(kk * kk, hsum_ref[...])
        kn_ref[0, :, sl] = (kk * lax.rsqrt(jnp.maximum(ss, 1e-24))).astype(BF16)
        k_ref[0, :, sl] = (k * (1.0 + (a - 1.0) * ka_ref[:, sl])).astype(BF16)


def _gate_proj_body(xn, w_ref, b_ref, o_ref):
    n = w_ref.shape[1]
    step = 4 * LANES
    for lo in range(0, n, step):
        y = _dot(xn, w_ref[:, lo:lo + step]) + b_ref[:, lo:lo + step]
        o_ref[0, :, lo:lo + step] = jax.nn.sigmoid(y).astype(BF16)


N_ATTN_IN, N_SC_IN, N_RW_IN, N_GATE_IN = 7, 2, 9, 2
N_ATTN_OUT, N_SC_OUT, N_RW_OUT = 5, 1, 7


def _in_proj_kernel(*refs):
    refs = list(refs)

    def take(n):
        taken, refs[:] = refs[:n], refs[n:]
        return taken

    h_ref, g_ref = take(2)
    attn_in, sc_in, rw_in, gate_in = (take(n) for n in (N_ATTN_IN, N_SC_IN, N_RW_IN, N_GATE_IN))
    attn_out, sc_out, rw_out = (take(n) for n in (N_ATTN_OUT, N_SC_OUT, N_RW_OUT))
    gate_out, sc_carry, rw_carry = take(3)

    @pl.when(pl.program_id(1) == 0)
    def _():
        sc_carry[...] = jnp.zeros_like(sc_carry)
        rw_carry[...] = jnp.zeros_like(rw_carry)

    xn = _rms(h_ref[0], g_ref[...])
    _attn_proj_body(xn, *attn_in, *attn_out)
    _sc_proj_body(xn, *sc_in, *sc_out, sc_carry)
    _rw_proj_body(xn, *rw_in, *rw_out, rw_carry)
    _gate_proj_body(xn, *gate_in, gate_out)


def _in_proj(h, g, attn_in, sc_in, rw_in, gate_in, ts):
    b, s, d = h.shape
    row = lambda i, j: (i, j, 0)
    const = lambda i, j: (0, 0)

    def seq_spec(width):
        return pl.BlockSpec((1, ts, width), row)

    def whole(x):
        if x.size * x.dtype.itemsize >= (1 << 20):
            return pl.BlockSpec(x.shape, const, pipeline_mode=pl.Buffered(1))
        return pl.BlockSpec(x.shape, const)

    w_attn, tabs = attn_in[0], attn_in[1:]
    in_specs = ([seq_spec(d), whole(g), whole(w_attn)] + [seq_spec(LANES)] * len(tabs)
                + [whole(x) for x in sc_in] + [whole(x) for x in rw_in]
                + [whole(x) for x in gate_in])
    out_widths = ([(A_WIDTH, BF16), (LANES, BF16), (4 * LANES, BF16), (IDX_WIDTH, BF16),
                   (LANES, F32), (SC_WIDTH, BF16)]
                  + [(RW_WIDTH, F32 if i == 1 else BF16) for i in range(N_RW_OUT)]
                  + [(gate_in[0].shape[1], BF16)])
    outs = pl.pallas_call(
        _in_proj_kernel,
        grid=(b, s // ts),
        in_specs=in_specs,
        out_specs=[seq_spec(w) for w, _ in out_widths],
        out_shape=[jax.ShapeDtypeStruct((b, s, w), dt) for w, dt in out_widths],
        scratch_shapes=[pltpu.VMEM((SUBLANES, SC_WIDTH), F32), pltpu.VMEM((SUBLANES, RW_IN), F32)],
        compiler_params=pltpu.CompilerParams(dimension_semantics=("arbitrary", "arbitrary"),
                                             vmem_limit_bytes=WIDE_VMEM_LIMIT),
        name="in_proj",
    )(h, g, *attn_in, *sc_in, *rw_in, *gate_in)
    return outs[:5], outs[5], outs[6:13], outs[13]


def _dsa_topk_bias(qi_ref, kwk_ref, kwq_ref, key_ref, hi_ref, lo_ref, bias_ref, *, t0, k_sel,
                   side_work):
    tq, sk = bias_ref.shape
    half_min = -(1 << 15)

    ki = kwk_ref[0][:, 0:IDX_DIM].astype(BF16)
    wq = kwq_ref[0]
    qi = qi_ref[0]
    isc = jnp.zeros((tq, sk), F32)
    for h in range(IDX_HEADS):
        rel = jnp.maximum(_dot_nt(qi[:, h * IDX_DIM:(h + 1) * IDX_DIM], ki), 0.0)
        isc = isc + wq[:, IDX_DIM + h:IDX_DIM + h + 1] * rel

    cols = lax.broadcasted_iota(jnp.int32, (tq, sk), 1)
    rows = t0 + lax.broadcasted_iota(jnp.int32, (tq, sk), 0)
    bits = pltpu.bitcast(isc, jnp.int32)
    key = bits ^ ((bits >> 31) & jnp.int32(0x7FFFFFFF))
    key = jnp.where(cols <= rows, key, INT_MIN)
    key_ref[...] = key
    hi_ref[...] = (key >> 16).astype(jnp.int16)
    lo_ref[...] = ((key & jnp.int32(0xFFFF)) + half_min).astype(jnp.int16)


    def search_half(src_ref, cnt0):
        tau_u, cnt_tau = jnp.zeros((tq, 1), jnp.int32), cnt0
        for it in range(16):
            cand_u = tau_u | (1 << (15 - it))
            cand = jnp.broadcast_to(cand_u + half_min, (tq, LANES)).astype(jnp.int16)
            accs = [jnp.zeros((tq, LANES), jnp.int16) for _ in range(4)]
            for j in range(sk // LANES):
                hit = src_ref[:, j * LANES:(j + 1) * LANES] >= cand
                accs[j % 4] = accs[j % 4] + jnp.where(hit, jnp.int16(1), jnp.int16(0))
            acc = (accs[0] + accs[1]) + (accs[2] + accs[3])
            cnt = jnp.sum(acc.astype(F32), axis=1, keepdims=True)
            ok = cnt >= k_sel
            tau_u, cnt_tau = jnp.where(ok, cand_u, tau_u), jnp.where(ok, cnt, cnt_tau)
            next(side_work, None)
        return tau_u, cnt_tau

    tau_hi_u, cnt_hi = search_half(hi_ref, jnp.full((tq, 1), float(sk), F32))
    tau_hi = tau_hi_u + half_min
    t_hi = jnp.broadcast_to(tau_hi, (tq, LANES)).astype(jnp.int16)
    for j in range(sk // LANES):
        sl = slice(j * LANES, (j + 1) * LANES)
        hi_j = hi_ref[:, sl]
        inside = jnp.where(hi_j == t_hi, lo_ref[:, sl], jnp.int16(half_min))
        lo_ref[:, sl] = jnp.where(hi_j > t_hi, jnp.int16(-half_min - 1), inside)
    tau_lo_u, cnt_ge = search_half(lo_ref, cnt_hi)
    tau = (tau_hi << 16) | tau_lo_u

    drop = jnp.where(tau == INT_MIN, float(2 * sk), cnt_ge - k_sel)
    r2 = lax.broadcasted_iota(jnp.int32, (LANES, 2 * LANES), 0)
    c2 = lax.broadcasted_iota(jnp.int32, (LANES, 2 * LANES), 1)
    tri = jnp.where((r2 > c2) | (c2 >= LANES), 1.0, 0.0).astype(BF16)
    after = jnp.zeros((tq, LANES), F32)
    for j in reversed(range(sk // LANES)):
        sl = slice(j * LANES, (j + 1) * LANES)
        key_j = key_ref[:, sl]
        eq = key_j == tau
        pr = _dot(jnp.where(eq, 1.0, 0.0).astype(BF16), tri)
        tied = jnp.where(eq, jnp.where(after + pr[:, :LANES] >= drop, 0.0, -jnp.inf), -jnp.inf)
        bias_ref[:, sl] = jnp.where(key_j > tau, 0.0, tied)
        after = after + pr[:, LANES:]


N_RW_STREAMS = 7
N_RW_REFS = N_RW_STREAMS + 4


def _dsa_rwkv_kernel(*refs, first_block, k_sel, rw_chunk):
    q_ref, k_ref, v_ref, qi_ref, kwk_ref, kwq_ref = refs[:6]
    rw_in = refs[6:6 + N_RW_REFS]
    o_ref, y_ref, st_out_ref, key_ref, hi_ref, lo_ref, bias_ref = refs[6 + N_RW_REFS + 2:]
    rwkv = _rwkv_stages(*rw_in, y_ref, st_out_ref, c=rw_chunk)
    tq = q_ref.shape[1]
    sk = k_ref.shape[1]
    t0 = first_block * tq
    if sk <= k_sel:
        cols = lax.broadcasted_iota(jnp.int32, (tq, sk), 1)
        rows = t0 + lax.broadcasted_iota(jnp.int32, (tq, sk), 0)
        bias_ref[...] = jnp.where(cols <= rows, 0.0, -jnp.inf)
    else:
        _dsa_topk_bias(qi_ref, kwk_ref, kwq_ref, key_ref, hi_ref, lo_ref, bias_ref,
                       t0=t0, k_sel=k_sel, side_work=rwkv)
    for _ in rwkv:
        pass

    group = A_HEADS // A_KV_HEADS
    low = lax.broadcasted_iota(jnp.int32, (tq, LANES), 1) < A_HEAD_DIM
    bias = bias_ref[...]
    kk = k_ref[0]
    for h2 in range(A_HEADS // 2):
        res = []
        for h in (2 * h2, 2 * h2 + 1):
            n = h // group
            kn = kk[:, n * A_HEAD_DIM:(n + 1) * A_HEAD_DIM]
            s = _dot_nt(q_ref[0, :, h * A_HEAD_DIM:(h + 1) * A_HEAD_DIM], kn) + bias
            m = jnp.max(s, axis=1, keepdims=True)
            p = jnp.exp2(s - m).astype(BF16)
            vx = v_ref[0, :, (2 * n + h % 2) * LANES:(2 * n + h % 2 + 1) * LANES]
            oe = _dot(p, vx)
            res.append(oe / pltpu.roll(oe, A_HEAD_DIM, 1))
        o_ref[0, :, h2 * LANES:(h2 + 1) * LANES] = jnp.where(low, res[0], res[1]).astype(BF16)


def _dsa_rwkv(q, k, v, qi, kw, k_sel, rw_streams, rw_vecs, out_bufs):
    b, s, _ = q.shape
    tq = min(DSA_Q_TILE, s)
    w = rw_streams[0].shape[2]
    c = min(RW_CHUNK, s)
    state = jnp.zeros((b, w // LANES, LANES, LANES), F32)
    ya, yc = out_bufs if out_bufs is not None else (jnp.zeros((b, s, A_WIDTH), BF16),
                                                    jnp.zeros((b, s, w), BF16))
    n_in = 6 + N_RW_REFS
    for first in range(s // tq):
        sk = (first + 1) * tq
        qmap = lambda i, j, first=first: (i, first, 0)
        kmap = lambda i, j: (i, 0, 0)
        st_spec = pl.BlockSpec((1, w // LANES, LANES, LANES), lambda i, j: (i, 0, 0, 0))
        ya, yc, state = pl.pallas_call(
            functools.partial(_dsa_rwkv_kernel, first_block=first, k_sel=k_sel, rw_chunk=c),
            grid=(b, 1),
            in_specs=[pl.BlockSpec((1, tq, A_WIDTH), qmap), pl.BlockSpec((1, sk, LANES), kmap),
                      pl.BlockSpec((1, sk, 4 * LANES), kmap),
                      pl.BlockSpec((1, tq, IDX_WIDTH), qmap),
                      pl.BlockSpec((1, sk, LANES), kmap), pl.BlockSpec((1, tq, LANES), qmap)]
                     + [pl.BlockSpec((1, tq, w), qmap)] * N_RW_STREAMS
                     + [pl.BlockSpec((1, w), lambda i, j: (0, 0))] * 3 + [st_spec]
                     + [pl.BlockSpec(memory_space=pl.ANY)] * 2,
            out_specs=[pl.BlockSpec((1, tq, A_WIDTH), qmap), pl.BlockSpec((1, tq, w), qmap),
                       st_spec],
            out_shape=[jax.ShapeDtypeStruct(ya.shape, BF16), jax.ShapeDtypeStruct(yc.shape, BF16),
                       jax.ShapeDtypeStruct(state.shape, F32)],
            input_output_aliases={n_in: 0, n_in + 1: 1},
            scratch_shapes=[pltpu.VMEM((tq, sk), jnp.int32), pltpu.VMEM((tq, sk), jnp.int16),
                            pltpu.VMEM((tq, sk), jnp.int16), pltpu.VMEM((tq, sk), F32)],
            compiler_params=_cparams(("arbitrary", "arbitrary")),
            name=f"dsa_rwkv_{sk}",
        )(q, k, v, qi, kw, kw, *rw_streams, *rw_vecs, state, ya, yc)
    return ya, yc


def _rwkv_stages(r_ref, lw_ref, k_ref, v_ref, kn_ref, a_ref, g_ref, rk_ref, lnw_ref, lnb_ref,
                 st_in_ref, y_ref, st_out_ref, *, c):
    width = r_ref.shape[2]
    nc = r_ref.shape[1] // c
    hd = RW_HEAD_DIM

    lw_all = lw_ref[0]
    rows = lax.broadcasted_iota(jnp.int32, lw_all.shape, 0) & (c - 1)
    cum_all = lw_all
    step = 1
    while step < c:
        cum_all = cum_all + jnp.where(rows >= step, pltpu.roll(cum_all, step, 0), 0.0)
        step *= 2

    lane = lax.broadcasted_iota(jnp.int32, (c, LANES), 1)
    first = lane < hd
    r2 = lax.broadcasted_iota(jnp.int32, (2 * c, 2 * c), 0)
    c2 = lax.broadcasted_iota(jnp.int32, (2 * c, 2 * c), 1)
    same_head = (r2 >= c) == (c2 >= c)
    t_row = r2 & (c - 1)
    t_col = c2 & (c - 1)
    strict = same_head & (t_col < t_row)
    lower = same_head & (t_col <= t_row)

    def blk(x):
        return jnp.concatenate([jnp.where(first, x, 0.0), jnp.where(first, 0.0, x)],
                               axis=0).astype(BF16)

    def head_sum(x):
        s0 = jnp.sum(jnp.where(first, x, 0.0), axis=1, keepdims=True)
        s1 = jnp.sum(jnp.where(first, 0.0, x), axis=1, keepdims=True)
        return jnp.where(first, s0, s1)

    pairs = range(width // LANES)
    sls = [slice(p * LANES, (p + 1) * LANES) for p in pairs]
    units = [(j, p) for j in range(nc) for p in pairs]
    n = 2 * c
    la, lr, vb, bk_t, bk_h, p_last, vals, bonus_in = {}, {}, {}, {}, {}, {}, {}, {}
    for j in range(nc):
        rs = slice(j * c, (j + 1) * c)
        r, kf, v = (x[0, rs, :].astype(F32) for x in (r_ref, k_ref, v_ref))
        kn, a = kn_ref[0, rs, :].astype(F32), a_ref[0, rs, :].astype(F32)
        lw, cum = lw_all[rs], cum_all[rs]
        cum_last = cum[c - 1:c, :]
        p_inv = jnp.exp(-cum)
        p_rem = jnp.exp(cum_last - cum)
        kb = kn * a
        a_t = -kn * jnp.exp(cum - lw)
        r_t = r * jnp.exp(cum)
        b_t = kb * p_inv
        k_t = kf * p_inv
        b_h = kb * p_rem
        k_h = kf * p_rem
        p_last[j] = jnp.exp(cum_last)
        vals[j] = v
        bonus_in[j] = r * kf * rk_ref[...]
        for p in pairs:
            sl = sls[p]
            la[j, p], lr[j, p], vb[j, p] = blk(a_t[:, sl]), blk(r_t[:, sl]), blk(v[:, sl])
            bk_t[j, p] = jnp.concatenate([blk(b_t[:, sl]), blk(k_t[:, sl])], axis=0)
            bk_h[j, p] = jnp.concatenate([blk(b_h[:, sl]), blk(k_h[:, sl])], axis=0)
        yield
    gram = {u: _dot_nt(jnp.concatenate([la[u], lr[u]], axis=0), bk_t[u]) for u in units}
    x = {u: jnp.where(strict, gram[u][:n, :n], 0.0) for u in units}
    npow = dict(x)
    yield
    for _ in range(int(math.log2(c)) - 1):
        npow_b = {u: npow[u].astype(BF16) for u in units}
        npow = {u: _dot(npow_b[u], npow_b[u]) for u in units}
        x = {u: x[u] + npow[u] + _dot(x[u].astype(BF16), npow[u].astype(BF16)) for u in units}
        yield
    x_b = {u: x[u].astype(BF16) for u in units}
    a_ak = {u: jnp.where(strict, gram[u][:n, n:], 0.0).astype(BF16) for u in units}
    lower2 = jnp.concatenate([lower, lower], axis=1)
    a_rbk = {u: jnp.where(lower2, gram[u][n:, :], 0.0).astype(BF16) for u in units}
    akv = {u: _dot(a_ak[u], vb[u]) for u in units}
    yield

    st = [st_in_ref[0, p] for p in pairs]
    for j in range(nc):
        st_b = [s.astype(BF16) for s in st]
        rhs = [_dot_nt(la[j, p], st_b[p]) + akv[j, p] for p in pairs]
        sa = [rhs[p] + _dot(x_b[j, p], rhs[p].astype(BF16)) for p in pairs]
        yield
        sav = [jnp.concatenate([sa[p].astype(BF16), vb[j, p]], axis=0) for p in pairs]
        yb = [_dot_nt(lr[j, p], st_b[p]) + _dot(a_rbk[j, p], sav[p]) for p in pairs]
        st = [st[p] * p_last[j][:, sls[p]] + _dot_tn(sav[p], bk_h[j, p]) for p in pairs]
        yield
        rs = slice(j * c, (j + 1) * c)
        for p in pairs:
            sl = sls[p]
            y = yb[p][:c] + yb[p][c:]
            mean = head_sum(y) * (1.0 / hd)
            yc = y - mean
            var = head_sum(yc * yc) * (1.0 / hd)
            yn = yc * lax.rsqrt(var + RW_GN_EPS) * lnw_ref[:, sl] + lnb_ref[:, sl]
            bonus = head_sum(bonus_in[j][:, sl]) * vals[j][:, sl]
            y_ref[0, rs, sl] = ((yn + bonus) * g_ref[0, rs, sl]).astype(BF16)
        yield
    for p in pairs:
        st_out_ref[0, p] = st[p]


def _merge_ffn_kernel(h_ref, ya_ref, yb_ref, yc_ref, gt_ref, wb_ref, wo_ref,
                      g_ref, wup_ref, cw_ref, wd_ref, gf_ref, o_ref, act_ref, carry_ref, *,
                      final_norm):
    @pl.when(pl.program_id(1) == 0)
    def _():
        carry_ref[...] = jnp.zeros_like(carry_ref)

    ts, d = h_ref.shape[1], h_ref.shape[2]
    dff = wd_ref.shape[0]
    mixed = None
    for i, y_ref in enumerate((ya_ref, yb_ref, yc_ref)):
        up = _dot(y_ref[0], wb_ref[i])
        term = gt_ref[0, :, i * d:(i + 1) * d].astype(F32) * up
        mixed = term if mixed is None else mixed + term
    h_mid = h_ref[0] + _dot(mixed.astype(BF16), wo_ref[...])
    xn = _rms(h_mid, g_ref[...])

    def conv(sl):
        u = _dot(xn, wup_ref[:, sl])
        prev = carry_ref[:, sl]
        out = (_shift_rows(u, prev, 2) * cw_ref[0:1, sl] + _shift_rows(u, prev, 1) * cw_ref[1:2, sl]
               + u * cw_ref[2:3, sl])
        carry_ref[:, sl] = u[ts - SUBLANES:ts]
        return out

    for lo in range(0, dff, MXU_COLS):
        gate = conv(slice(lo, lo + MXU_COLS))
        up = conv(slice(dff + lo, dff + lo + MXU_COLS))
        act_ref[:, lo:lo + MXU_COLS] = (gate * jax.nn.sigmoid(gate) * up).astype(BF16)
    out = h_mid + _dot(act_ref[...], wd_ref[...])
    if final_norm:
        ms = jnp.mean(out * out, axis=-1, keepdims=True)
        out = out * lax.rsqrt(ms + RMS_EPS) * gf_ref[...]
    o_ref[0] = out


def _merge_ffn(h, ya, yb, yc, gates, wb, wo, g, w_up, w_conv, w_down, g_final, final_norm, ts):
    b, s, d = h.shape
    dff = w_down.shape[0]
    bw = ya.shape[2]
    row = lambda i, j: (i, j, 0)
    const = lambda i, j: (0, 0)
    resident = dict(pipeline_mode=pl.Buffered(1))
    return pl.pallas_call(
        functools.partial(_merge_ffn_kernel, final_norm=final_norm),
        grid=(b, s // ts),
        in_specs=[pl.BlockSpec((1, ts, d), row)] + [pl.BlockSpec((1, ts, bw), row)] * 3
                 + [pl.BlockSpec((1, ts, N_BRANCH * d), row),
                    pl.BlockSpec((N_BRANCH, bw, d), lambda i, j: (0, 0, 0), **resident),
                    pl.BlockSpec((d, d), const, **resident),
                    pl.BlockSpec((1, d), const),
                    pl.BlockSpec((d, 2 * dff), const, **resident),
                    pl.BlockSpec((3, 2 * dff), const),
                    pl.BlockSpec((dff, d), const, **resident),
                    pl.BlockSpec((1, d), const)],
        out_specs=pl.BlockSpec((1, ts, d), row),
        out_shape=jax.ShapeDtypeStruct((b, s, d), F32),
        scratch_shapes=[pltpu.VMEM((ts, dff), BF16), pltpu.VMEM((SUBLANES, 2 * dff), F32)],
        compiler_params=pltpu.CompilerParams(dimension_semantics=("arbitrary", "arbitrary"),
                                             vmem_limit_bytes=WIDE_VMEM_LIMIT),
        name="merge_ffn",
    )(h, ya, yb, yc, gates, wb, wo, g, w_up, w_conv, w_down, g_final)


def _rope_tables(positions, head_dim):
    rot = head_dim // ROPE_FRACTION
    half = rot // 2
    inv = ROPE_THETA ** (-jnp.arange(half, dtype=F32) * 2.0 / rot)
    ang = positions.astype(F32)[:, :, None] * inv
    cos, sin = jnp.cos(ang), jnp.sin(ang)
    zeros = jnp.zeros_like(cos)
    rest = head_dim - rot
    pad1 = jnp.ones(cos.shape[:2] + (rest,), F32)
    pad0 = jnp.zeros(cos.shape[:2] + (rest,), F32)
    reps = LANES // head_dim
    c = jnp.tile(jnp.concatenate([cos, cos, pad1], axis=-1), (1, 1, reps))
    s1 = jnp.tile(jnp.concatenate([-sin, zeros, pad0], axis=-1), (1, 1, reps))
    s2 = jnp.tile(jnp.concatenate([zeros, sin, pad0], axis=-1), (1, 1, reps))
    return c, s1, s2


def kernel(x, positions, norm_mix, w_in, b_gate, sc_conv, rw_mu, rw_w0, rw_w_up, rw_a0, rw_a_up, rw_g_up, rw_k_k, rw_k_a, rw_r_k, rw_ln_w, rw_ln_b, w_branch, w_out, norm_ffn, ffn_up, ffn_conv, ffn_down, norm_final):
    bsz, seq, d = x.shape
    depth = w_in.shape[0]
    ts = min(ROW_TILE, seq)
    k_sel = min(TOPK_MAX, seq // 4)

    tabs = _rope_tables(positions, A_HEAD_DIM) + _rope_tables(positions, IDX_DIM)
    o_q = 0
    o_k = o_q + A_WIDTH
    o_v = o_k + A_KV_WIDTH
    o_qi = o_v + A_KV_WIDTH
    o_ki = o_qi + IDX_WIDTH
    o_wi = o_ki + IDX_DIM
    o_sc = o_wi + IDX_HEADS
    o_rw = o_sc + 3 * SC_WIDTH
    o_gate = o_rw + RW_IN
    attn_scale = A_HEAD_DIM ** -0.5 * math.log2(math.e)
    idx_scale = (IDX_HEADS ** -0.5) * (IDX_DIM ** -0.5)
    head_sum = jnp.kron(jnp.eye(MXU_COLS // RW_HEAD_DIM, dtype=F32),
                        jnp.ones((RW_HEAD_DIM, RW_HEAD_DIM), F32)).astype(BF16)

    h = x
    mixer_bufs = None
    for l in range(depth):
        w = w_in[l]
        pad = jnp.zeros((d, LANES - IDX_DIM - IDX_HEADS), F32)
        w_attn = jnp.concatenate(
            [w[:, o_q:o_k] * attn_scale, w[:, o_k:o_ki], w[:, o_ki:o_wi],
             w[:, o_wi:o_sc] * idx_scale, pad], axis=1).astype(BF16)
        w_sc = w[:, o_sc:o_rw].astype(BF16)
        w_rw = w[:, o_rw:o_gate].astype(BF16)
        w_gate = w[:, o_gate:].astype(BF16)
        g_mix = norm_mix[l][None, :]

        zero_lora = jnp.zeros((RW_W_LORA, RW_WIDTH), F32)
        w_lora = jnp.concatenate(
            [jnp.concatenate([rw_w_up[l], zero_lora], axis=1),
             jnp.concatenate([zero_lora, rw_a_up[l]], axis=1)], axis=0).astype(BF16)
        (q, k, v, qi, kw), y_b, (r, lw, kf, vv, kn, a, gg), gates = _in_proj(
            h, g_mix,
            (w_attn,) + tabs,
            (w_sc, sc_conv[l]),
            (w_rw, rw_mu[l][None, :], w_lora, rw_g_up[l].astype(BF16), rw_w0[l][None, :],
             rw_a0[l][None, :], rw_k_k[l][None, :], rw_k_a[l][None, :], head_sum),
            (w_gate, b_gate[l][None, :]), ts)

        y_a, y_c = mixer_bufs = _dsa_rwkv(
            q, k, v, qi, kw, k_sel, (r, lw, kf, vv, kn, a, gg),
            (rw_r_k[l].reshape(1, RW_WIDTH), rw_ln_w[l][None, :], rw_ln_b[l][None, :]),
            mixer_bufs)

        h = _merge_ffn(h, y_a, y_b, y_c, gates, w_branch[l].astype(BF16), w_out[l].astype(BF16),
                       norm_ffn[l][None, :], ffn_up[l].astype(BF16), ffn_conv[l],
                       ffn_down[l].astype(BF16), norm_final[None, :], l == depth - 1, ts)
    return h
```
